```python
import math
import jax, jax.numpy as jnp
from jax import lax
import numpy as np

D_MODEL = 1024
BATCH = 1
SEQ = 16384
DEPTH = 1
DEC_BATCH = 8
DEC_SEQ = 2048
PAST_LEN = 128

FOURIER_WIDTH = 512
FOURIER_GROUPS = 4
FOURIER_GROUP_DIM = FOURIER_WIDTH // FOURIER_GROUPS
ATTN_HEADS = 4
ATTN_HEAD_DIM = 64
ATTN_QK_WIDTH = ATTN_HEADS * 2 * ATTN_HEAD_DIM
ATTN_V_DIM = 2 * ATTN_HEAD_DIM
ATTN_V_WIDTH = ATTN_HEADS * ATTN_V_DIM
N_BRANCHES = 2
IN_WIDTH = FOURIER_WIDTH + 2 * ATTN_QK_WIDTH + ATTN_V_WIDTH + N_BRANCHES * D_MODEL
ROPE_THETA = 10000.0
Q_BLOCK = 128
N_GROUPS = 4
EXPERTS_PER_GROUP = 8
N_EXPERTS = N_GROUPS * EXPERTS_PER_GROUP
TOP_K = 2
EXPERT_FF = 1024
MOE_BLOCK = 128
NORM_EPS = 1e-6

kernel_name = 'hybrid_fnet_diffattn_hmoe_encoder'


def rms_norm(x, w):
    x32 = x.astype(jnp.float32)
    y = x32 * lax.rsqrt(jnp.mean(x32 * x32, axis=-1, keepdims=True) + NORM_EPS)
    return (y * w.astype(jnp.float32)).astype(x.dtype)


def rope_tables(S):
    pos = jnp.arange(S, dtype=jnp.float32)
    inv_freq = ROPE_THETA ** (-jnp.arange(0, ATTN_HEAD_DIM, 2, dtype=jnp.float32) / ATTN_HEAD_DIM)
    ang = pos[:, None] * inv_freq[None, :]
    return jnp.cos(ang), jnp.sin(ang)


def apply_rope(t, cos, sin):
    t32 = t.astype(jnp.float32)
    half = ATTN_HEAD_DIM // 2
    t1, t2 = t32[..., :half], t32[..., half:]
    c = cos[None, :, None, None, :]
    s = sin[None, :, None, None, :]
    return jnp.concatenate([t1 * c - t2 * s, t2 * c + t1 * s], axis=-1).astype(t.dtype)


def fourier_mix(f):
    B, S, _ = f.shape
    fg = f.reshape(B, S, FOURIER_GROUPS, FOURIER_GROUP_DIM).astype(jnp.float32)
    out = jnp.fft.fftn(fg, axes=(1, 3), norm='ortho').real
    return out.reshape(B, S, FOURIER_WIDTH).astype(f.dtype)


def diff_attention(q, k, v, lam, lambda_init, subln_w):
    B, S = q.shape[0], q.shape[1]
    nq = S // Q_BLOCK
    scale = ATTN_HEAD_DIM ** -0.5
    qb = q.reshape(B, nq, Q_BLOCK, ATTN_HEADS, 2, ATTN_HEAD_DIM).swapaxes(0, 1)

    def one_block(qblk):
        s = jnp.einsum('bqhcd,bkhcd->bhcqk', qblk, k, preferred_element_type=jnp.float32) * scale
        p = jax.nn.softmax(s, axis=-1)
        a = p[:, :, 0] - lam * p[:, :, 1]
        return jnp.einsum('bhqk,bkhe->bqhe', a.astype(v.dtype), v, preferred_element_type=jnp.float32)

    o = lax.map(one_block, qb)
    o = o.swapaxes(0, 1).reshape(B, S, ATTN_HEADS, ATTN_V_DIM)
    o = o * lax.rsqrt(jnp.mean(o * o, axis=-1, keepdims=True) + NORM_EPS)
    o = o * subln_w.astype(jnp.float32) * (1.0 - lambda_init)
    return o.reshape(B, S, ATTN_V_WIDTH).astype(q.dtype)


def token_mixer(xn, w_in, b_gate, w_fourier, lambda_q1, lambda_k1, lambda_q2, lambda_k2,
                subln_w, w_attn, w_out, lambda_init, cos, sin):
    B, S, _ = xn.shape
    proj = xn @ w_in
    o1 = FOURIER_WIDTH
    o2 = o1 + ATTN_QK_WIDTH
    o3 = o2 + ATTN_QK_WIDTH
    o4 = o3 + ATTN_V_WIDTH
    f = proj[..., :o1]
    q = proj[..., o1:o2].reshape(B, S, ATTN_HEADS, 2, ATTN_HEAD_DIM)
    k = proj[..., o2:o3].reshape(B, S, ATTN_HEADS, 2, ATTN_HEAD_DIM)
    v = proj[..., o3:o4].reshape(B, S, ATTN_HEADS, ATTN_V_DIM)
    gates = jax.nn.sigmoid((proj[..., o4:] + b_gate).astype(jnp.float32)).reshape(B, S, N_BRANCHES, D_MODEL)

    branch_f = fourier_mix(f) @ w_fourier

    q = apply_rope(q, cos, sin)
    k = apply_rope(k, cos, sin)
    lq1 = lambda_q1.astype(jnp.float32)
    lk1 = lambda_k1.astype(jnp.float32)
    lq2 = lambda_q2.astype(jnp.float32)
    lk2 = lambda_k2.astype(jnp.float32)
    lam = jnp.exp(jnp.sum(lq1 * lk1)) - jnp.exp(jnp.sum(lq2 * lk2)) + lambda_init
    branch_a = diff_attention(q, k, v, lam, lambda_init, subln_w) @ w_attn

    merged = gates[..., 0, :] * branch_f.astype(jnp.float32) + gates[..., 1, :] * branch_a.astype(jnp.float32)
    return merged.astype(xn.dtype) @ w_out


def hier_moe(xn, w_group_router, w_expert_router, w_gate, w_up, w_down):
    B, S, D = xn.shape
    T = B * S
    xt = xn.reshape(T, D)
    x32 = xt.astype(jnp.float32)
    group_logits = x32 @ w_group_router.astype(jnp.float32)
    group_probs = jax.nn.softmax(group_logits, axis=-1)
    g_idx = jnp.argmax(group_logits, axis=-1).astype(jnp.int32)
    g_w = jnp.take_along_axis(group_probs, g_idx[:, None], axis=1)[:, 0]
    expert_logits = (x32 @ w_expert_router.astype(jnp.float32)).reshape(T, N_GROUPS, EXPERTS_PER_GROUP)
    in_group = jnp.take_along_axis(expert_logits, g_idx[:, None, None], axis=1)[:, 0]
    top_v, top_i = lax.top_k(in_group, TOP_K)
    weights = g_w[:, None] * jax.nn.softmax(top_v, axis=-1)
    expert = g_idx[:, None] * EXPERTS_PER_GROUP + top_i.astype(jnp.int32)

    A = T * TOP_K
    e_flat = expert.reshape(A)
    w_flat = weights.reshape(A)
    tok_flat = jnp.arange(A, dtype=jnp.int32) // TOP_K
    order = jnp.argsort(e_flat)
    e_sorted = e_flat[order]
    counts = jax.ops.segment_sum(jnp.ones((A,), jnp.int32), e_flat, num_segments=N_EXPERTS)
    padded = (counts + MOE_BLOCK - 1) // MOE_BLOCK * MOE_BLOCK
    padded_end = jnp.cumsum(padded)
    pad_start = padded_end - padded
    start = jnp.cumsum(counts) - counts
    dest = pad_start[e_sorted] + jnp.arange(A, dtype=jnp.int32) - start[e_sorted]
    n_blocks = -(-(A + N_EXPERTS * (MOE_BLOCK - 1)) // MOE_BLOCK)
    P = n_blocks * MOE_BLOCK
    row_tok = jnp.full((P,), T, jnp.int32).at[dest].set(tok_flat[order])
    row_w = jnp.zeros((P,), jnp.float32).at[dest].set(w_flat[order])
    block_start = jnp.arange(n_blocks, dtype=jnp.int32) * MOE_BLOCK
    block_expert = jnp.minimum(jnp.searchsorted(padded_end, block_start, side='right'), N_EXPERTS - 1).astype(jnp.int32)

    x_pad = jnp.concatenate([xt, jnp.zeros((1, D), xt.dtype)], axis=0)
    xb = x_pad[row_tok].reshape(n_blocks, MOE_BLOCK, D)

    def expert_block(args):
        xblk, e = args
        h = jax.nn.silu(xblk @ w_gate[e]) * (xblk @ w_up[e])
        return h @ w_down[e]

    yb = lax.map(expert_block, (xb, block_expert)).reshape(P, D)
    y = jnp.zeros((T + 1, D), jnp.float32).at[row_tok].add(yb.astype(jnp.float32) * row_w[:, None])[:T]
    return y.reshape(B, S, D).astype(xn.dtype)


def encoder(x, norm_mix_w, w_in, b_gate, w_fourier, lambda_q1, lambda_k1, lambda_q2, lambda_k2,
            subln_w, w_attn, w_out, norm_ffn_w, w_group_router, w_expert_router,
            w_expert_gate, w_expert_up, w_expert_down, norm_final_w):
    S = x.shape[1]
    cos, sin = rope_tables(S)
    h = x
    for layer in range(DEPTH):
        lambda_init = 0.8 - 0.6 * math.exp(-0.3 * layer)
        h = h + token_mixer(rms_norm(h, norm_mix_w[layer]), w_in[layer], b_gate[layer], w_fourier[layer],
                            lambda_q1[layer], lambda_k1[layer], lambda_q2[layer], lambda_k2[layer],
                            subln_w[layer], w_attn[layer], w_out[layer], lambda_init, cos, sin)
        h = h + hier_moe(rms_norm(h, norm_ffn_w[layer]), w_group_router[layer], w_expert_router[layer],
                         w_expert_gate[layer], w_expert_up[layer], w_expert_down[layer])
    return rms_norm(h, norm_final_w)


def setup_inputs(seed: int = 0) -> dict:
    key = jax.random.key(seed)
    ks = jax.random.split(key, 24)
    f32 = jnp.float32
    D = D_MODEL
    L = DEPTH

    def nrm(k, shape, scale):
        return jax.random.normal(k, shape, f32) * scale

    return {
        'x_prompt': nrm(ks[0], (BATCH, SEQ, D), 1.0),
        'x_sample': nrm(ks[1], (DEC_BATCH, DEC_SEQ, D), 1.0),
        'norm_mix_w': 1.0 + nrm(ks[2], (L, D), 0.02),
        'w_in': nrm(ks[3], (L, D, IN_WIDTH), D ** -0.5),
        'b_gate': nrm(ks[4], (L, N_BRANCHES * D), 0.02),
        'w_fourier': nrm(ks[5], (L, FOURIER_WIDTH, D), FOURIER_WIDTH ** -0.5),
        'lambda_q1': nrm(ks[6], (L, ATTN_HEAD_DIM), 0.1),
        'lambda_k1': nrm(ks[7], (L, ATTN_HEAD_DIM), 0.1),
        'lambda_q2': nrm(ks[8], (L, ATTN_HEAD_DIM), 0.1),
        'lambda_k2': nrm(ks[9], (L, ATTN_HEAD_DIM), 0.1),
        'subln_w': 1.0 + nrm(ks[10], (L, ATTN_V_DIM), 0.02),
        'w_attn': nrm(ks[11], (L, ATTN_V_WIDTH, D), ATTN_V_WIDTH ** -0.5),
        'w_out': nrm(ks[12], (L, D, D), D ** -0.5),
        'norm_ffn_w': 1.0 + nrm(ks[13], (L, D), 0.02),
        'w_group_router': nrm(ks[14], (L, D, N_GROUPS), D ** -0.5),
        'w_expert_router': nrm(ks[15], (L, D, N_EXPERTS), D ** -0.5),
        'w_expert_gate': nrm(ks[16], (L, N_EXPERTS, D, EXPERT_FF), D ** -0.5),
        'w_expert_up': nrm(ks[17], (L, N_EXPERTS, D, EXPERT_FF), D ** -0.5),
        'w_expert_down': nrm(ks[18], (L, N_EXPERTS, EXPERT_FF, D), EXPERT_FF ** -0.5),
        'norm_final_w': 1.0 + nrm(ks[19], (D,), 0.02),
    }


def reference(x_prompt, x_sample, norm_mix_w, w_in, b_gate, w_fourier, lambda_q1, lambda_k1, lambda_q2, lambda_k2,
              subln_w, w_attn, w_out, norm_ffn_w, w_group_router, w_expert_router,
              w_expert_gate, w_expert_up, w_expert_down, norm_final_w):
    y_prompt = encoder(x_prompt, norm_mix_w, w_in, b_gate, w_fourier, lambda_q1, lambda_k1, lambda_q2, lambda_k2,
                       subln_w, w_attn, w_out, norm_ffn_w, w_group_router, w_expert_router,
                       w_expert_gate, w_expert_up, w_expert_down, norm_final_w)
    y_sample = encoder(x_sample, norm_mix_w, w_in, b_gate, w_fourier, lambda_q1, lambda_k1, lambda_q2, lambda_k2,
                       subln_w, w_attn, w_out, norm_ffn_w, w_group_router, w_expert_router,
                       w_expert_gate, w_expert_up, w_expert_down, norm_final_w)
    return (y_prompt, y_sample)
```

```python
import functools
import math

import numpy as np
import jax
import jax.numpy as jnp
from jax import lax
from jax.experimental import pallas as pl
from jax.experimental.pallas import tpu as pltpu

F32 = jnp.float32
BF16 = jnp.bfloat16
U32 = jnp.uint32
I32 = jnp.int32

D_MODEL = 1024
FOURIER_WIDTH = 512
FOURIER_GROUPS = 4
GROUP_DIM = 128
HEADS = 4
HEAD_DIM = 64
QK_WIDTH = 512
V_DIM = 128
V_WIDTH = 512
ROPE_THETA = 10000.0
N_GROUPS = 4
EXPERTS_PER_GROUP = 8
N_EXPERTS = 32
EXPERT_FF = 1024
NORM_EPS = 1e-6
LAMBDA_INIT = 0.8 - 0.6 * math.exp(0.0)

LANES = 128
ROUTER_LANES = 128
EXPERT_LANE0 = N_GROUPS
MOE_ROWS = 256
DFT_N = 128
VMEM_LIMIT = 56 * 1024 * 1024

LOG2E = 1.4426950408889634
NEG_BIG = -3.0e38


def _cparams(sem, vmem=VMEM_LIMIT):
    return pltpu.CompilerParams(dimension_semantics=sem, vmem_limit_bytes=vmem)


def _swap_halves(t):
    n = t.shape[1]
    lane = lax.broadcasted_iota(I32, t.shape, 1)
    first = (lane % HEAD_DIM) < (HEAD_DIM // 2)
    return jnp.where(first, pltpu.roll(t, n - HEAD_DIM // 2, 1), pltpu.roll(t, HEAD_DIM // 2, 1))


def _inproj_kernel(x_ref, nw_ref, w_ref, cos_ref, sin_ref, f_ref, qz_ref, k_ref, vt_ref):
    x = x_ref[...]
    xn = x * lax.rsqrt(jnp.mean(x * x, axis=-1, keepdims=True) + NORM_EPS) * nw_ref[...]
    proj = jnp.dot(xn.astype(BF16), w_ref[...], preferred_element_type=F32)
    f_ref[...] = proj[:, :FOURIER_WIDTH].astype(BF16)
    q = proj[:, FOURIER_WIDTH:FOURIER_WIDTH + QK_WIDTH]
    k = proj[:, FOURIER_WIDTH + QK_WIDTH:FOURIER_WIDTH + 2 * QK_WIDTH]
    v = proj[:, FOURIER_WIDTH + 2 * QK_WIDTH:]
    cos = jnp.concatenate([cos_ref[...]] * (QK_WIDTH // LANES), axis=1)
    sin = jnp.concatenate([sin_ref[...]] * (QK_WIDTH // LANES), axis=1)
    q = q * cos + _swap_halves(q) * sin
    k = k * cos + _swap_halves(k) * sin
    k_ref[...] = k.astype(BF16)
    q = q * (HEAD_DIM ** -0.5 * LOG2E)
    lane = lax.broadcasted_iota(I32, (q.shape[0], LANES), 1)
    lo = lane < HEAD_DIM
    for h in range(HEADS):
        qh = q[:, h * LANES:(h + 1) * LANES]
        qz_ref[:, (2 * h) * LANES:(2 * h + 1) * LANES] = jnp.where(lo, qh, 0.0).astype(BF16)
        qz_ref[:, (2 * h + 1) * LANES:(2 * h + 2) * LANES] = jnp.where(lo, 0.0, qh).astype(BF16)
    vt_ref[0] = v.T.astype(BF16)


def _inproj(x, nw, w1, cos, sin, tm):
    te = x.shape[0]
    nt = te // tm
    return pl.pallas_call(
        _inproj_kernel,
        grid=(nt,),
        in_specs=[
            pl.BlockSpec((tm, D_MODEL), lambda i: (i, 0)),
            pl.BlockSpec((1, D_MODEL), lambda i: (0, 0)),
            pl.BlockSpec((D_MODEL, 2048), lambda i: (0, 0)),
            pl.BlockSpec((tm, LANES), lambda i: (i, 0)),
            pl.BlockSpec((tm, LANES), lambda i: (i, 0)),
        ],
        out_specs=[
            pl.BlockSpec((tm, FOURIER_WIDTH), lambda i: (i, 0)),
            pl.BlockSpec((tm, 2 * QK_WIDTH), lambda i: (i, 0)),
            pl.BlockSpec((tm, QK_WIDTH), lambda i: (i, 0)),
            pl.BlockSpec((1, V_WIDTH, tm), lambda i: (i, 0, 0)),
        ],
        out_shape=[
            jax.ShapeDtypeStruct((te, FOURIER_WIDTH), BF16),
            jax.ShapeDtypeStruct((te, 2 * QK_WIDTH), BF16),
            jax.ShapeDtypeStruct((te, QK_WIDTH), BF16),
            jax.ShapeDtypeStruct((nt, V_WIDTH, tm), BF16),
        ],
        compiler_params=_cparams(("parallel",)),
        name="inproj",
    )(x, nw, w1, cos, sin)


@functools.lru_cache(maxsize=None)
def _dft_tables(batch, seq):
    n1 = seq // DFT_N
    assert batch * n1 == DFT_N
    idx = np.arange(n1)
    ang = 2.0 * np.pi * ((idx[:, None] * idx[None, :]) % n1) / n1
    eye = np.eye(batch)
    m1 = np.concatenate([np.kron(eye, np.cos(ang)), -np.kron(eye, np.sin(ang))], axis=0)
    k1 = np.arange(n1)[:, None, None]
    k2 = np.arange(DFT_N)[None, :, None]
    n2 = np.arange(DFT_N)[None, None, :]
    ang2 = 2.0 * np.pi * ((n2 * (k1 + n1 * k2)) % seq) / seq
    er, ei = np.cos(ang2), -np.sin(ang2)
    m2 = np.concatenate([np.concatenate([er, -ei], axis=2), np.concatenate([ei, er], axis=2)], axis=1)
    c = np.arange(GROUP_DIM)
    ang3 = 2.0 * np.pi * ((c[:, None] * c[None, :]) % GROUP_DIM) / GROUP_DIM
    norm = 1.0 / math.sqrt(seq * GROUP_DIM)
    cs = np.concatenate([np.cos(ang3), np.sin(ang3)], axis=0) * norm
    return (jnp.asarray(m1, BF16), jnp.asarray(m2, BF16), jnp.asarray(cs, BF16))


def _dft1_kernel(m_ref, x_ref, y_ref):
    y_ref[...] = jnp.dot(m_ref[...], x_ref[...], preferred_element_type=F32).astype(BF16)


def _dft1(m1, x2d, tn=4096):
    n = x2d.shape[1]
    return pl.pallas_call(
        _dft1_kernel,
        grid=(n // tn,),
        in_specs=[pl.BlockSpec((2 * DFT_N, DFT_N), lambda i: (0, 0)),
                  pl.BlockSpec((DFT_N, tn), lambda i: (0, i))],
        out_specs=pl.BlockSpec((2 * DFT_N, tn), lambda i: (0, i)),
        out_shape=jax.ShapeDtypeStruct((2 * DFT_N, n), BF16),
        compiler_params=_cparams(("parallel",)),
        name="dft_stage1",
    )(m1, x2d)


def _dft2_kernel(m_ref, yr_ref, yi_ref, cs_ref, o_ref, *, rows_per_step):
    for j in range(rows_per_step):
        m = m_ref[j]
        z = (jnp.dot(m[:, :DFT_N], yr_ref[0, j], preferred_element_type=F32)
             + jnp.dot(m[:, DFT_N:], yi_ref[0, j], preferred_element_type=F32))
        zb = z.astype(BF16)
        for g in range(FOURIER_GROUPS):
            zg = jnp.concatenate([zb[:DFT_N, g * LANES:(g + 1) * LANES],
                                  zb[DFT_N:, g * LANES:(g + 1) * LANES]], axis=1)
            o_ref[0, :, (j * FOURIER_GROUPS + g) * LANES:(j * FOURIER_GROUPS + g + 1) * LANES] = (
                jnp.dot(zg, cs_ref[...], preferred_element_type=F32).astype(BF16))


def _dft2(m2, y, cs, batch, seq, rows_per_step=4):
    n1 = seq // DFT_N
    r = rows_per_step
    steps = DFT_N // r
    per_b = n1 // r
    return pl.pallas_call(
        functools.partial(_dft2_kernel, rows_per_step=r),
        grid=(steps,),
        in_specs=[
            pl.BlockSpec((r, 2 * DFT_N, 2 * DFT_N), lambda i: (i % per_b, 0, 0)),
            pl.BlockSpec((1, r, DFT_N, FOURIER_WIDTH), lambda i: (0, i, 0, 0)),
            pl.BlockSpec((1, r, DFT_N, FOURIER_WIDTH), lambda i: (1, i, 0, 0)),
            pl.BlockSpec((2 * GROUP_DIM, GROUP_DIM), lambda i: (0, 0)),
        ],
        out_specs=pl.BlockSpec((1, DFT_N, r * FOURIER_WIDTH), lambda i: (i // per_b, 0, i % per_b)),
        out_shape=jax.ShapeDtypeStruct((batch, DFT_N, n1 * FOURIER_WIDTH), BF16),
        compiler_params=_cparams(("parallel",)),
        name="dft_stage2",
    )(m2, y, y, cs)


def _fourier(f, batch, seq):
    m1, m2, cs = _dft_tables(batch, seq)
    y = _dft1(m1, f.reshape(DFT_N, DFT_N * FOURIER_WIDTH))
    out = _dft2(m2, y.reshape(2, DFT_N, DFT_N, FOURIER_WIDTH), cs, batch, seq)
    return out.reshape(batch * seq, FOURIER_WIDTH)


def _attn_kernel(qz_ref, k_ref, vt_ref, lam_ref, sub_ref, o_ref, m_sc, l_sc, acc_sc, *, ck, n_chunks):
    j = pl.program_id(2)

    @pl.when(j == 0)
    def _():
        m_sc[...] = jnp.full(m_sc.shape, NEG_BIG, F32)
        l_sc[...] = jnp.zeros(l_sc.shape, F32)
        acc_sc[...] = jnp.zeros(acc_sc.shape, F32)

    def chunk(c, carry):
        row0 = pl.multiple_of(c * ck, ck)
        for h in range(HEADS):
            kh = k_ref[pl.ds(row0, ck), h * LANES:(h + 1) * LANES]
            vth = vt_ref[c, h * V_DIM:(h + 1) * V_DIM, :]
            for cc in range(2):
                p_i = 2 * h + cc
                qw = qz_ref[:, p_i * LANES:(p_i + 1) * LANES]
                s = lax.dot_general(kh, qw, (((1,), (1,)), ((), ())),
                                    preferred_element_type=F32)
                m_old = m_sc[p_i]
                m_new = jnp.maximum(m_old, jnp.max(s, axis=0, keepdims=True))
                p = jnp.exp2(s - m_new)
                alpha = jnp.exp2(m_old - m_new)
                l_sc[p_i] = alpha * l_sc[p_i] + jnp.sum(p, axis=0, keepdims=True)
                acc_sc[p_i] = alpha * acc_sc[p_i] + jnp.dot(vth, p.astype(BF16),
                                                            preferred_element_type=F32)
                m_sc[p_i] = m_new
        return carry

    lax.fori_loop(0, n_chunks, chunk, 0)

    @pl.when(j == pl.num_programs(2) - 1)
    def _():
        lv = lam_ref[...]
        lam = (jnp.exp(jnp.sum(lv[0:1] * lv[1:2], axis=1, keepdims=True))
               - jnp.exp(jnp.sum(lv[2:3] * lv[3:4], axis=1, keepdims=True)) + LAMBDA_INIT)
        for h in range(HEADS):
            o = acc_sc[2 * h] / l_sc[2 * h] - lam * (acc_sc[2 * h + 1] / l_sc[2 * h + 1])
            o = o * lax.rsqrt(jnp.mean(o * o, axis=0, keepdims=True) + NORM_EPS)
            o = o * sub_ref[...] * (1.0 - LAMBDA_INIT)
            o_ref[:, h * V_DIM:(h + 1) * V_DIM] = o.T.astype(BF16)


def _attention(qz, k, vt, lam_vecs, sub_col, batch, seq, tq, ck, kv_block):
    nq = seq // tq
    nkb = seq // kv_block
    n_chunks = kv_block // ck
    return pl.pallas_call(
        functools.partial(_attn_kernel, ck=ck, n_chunks=n_chunks),
        grid=(batch, nq, nkb),
        in_specs=[
            pl.BlockSpec((tq, 2 * QK_WIDTH), lambda b, i, j: (b * nq + i, 0)),
            pl.BlockSpec((kv_block, QK_WIDTH), lambda b, i, j: (b * nkb + j, 0)),
            pl.BlockSpec((n_chunks, V_WIDTH, ck), lambda b, i, j: (b * nkb + j, 0, 0)),
            pl.BlockSpec((4, HEAD_DIM), lambda b, i, j: (0, 0)),
            pl.BlockSpec((V_DIM, 1), lambda b, i, j: (0, 0)),
        ],
        out_specs=pl.BlockSpec((tq, V_WIDTH), lambda b, i, j: (b * nq + i, 0)),
        out_shape=jax.ShapeDtypeStruct((batch * seq, V_WIDTH), BF16),
        scratch_shapes=[
            pltpu.VMEM((2 * HEADS, 1, tq), F32),
            pltpu.VMEM((2 * HEADS, 1, tq), F32),
            pltpu.VMEM((2 * HEADS, V_DIM, tq), F32),
        ],
        compiler_params=_cparams(("parallel", "parallel", "arbitrary")),
        name="diff_attention",
    )(qz, k, vt, lam_vecs, sub_col)


def _pack_bf16_pairs(a, b):
    ab = lax.bitcast_convert_type(a.astype(BF16).astype(F32), U32)
    bb = lax.bitcast_convert_type(b.astype(BF16).astype(F32), U32)
    return (bb & jnp.uint32(0xFFFF0000)) | (ab >> 16)


def _unpack_bf16_pairs(w):
    lo = lax.bitcast_convert_type(w << 16, F32)
    hi = lax.bitcast_convert_type(w & jnp.uint32(0xFFFF0000), F32)
    return lo, hi


def _merge_kernel(x_ref, four_ref, attn_ref, nmix_ref, wg_ref, bg_ref, wf_ref, wa_ref, wo_ref,
                  nffn_ref, wr_ref, cnt_in_ref,
                  h_ref, hnp_ref, ri_ref, rw_ref, cnt_ref, carry_sc):
    i = pl.program_id(0)

    @pl.when(i == 0)
    def _():
        carry_sc[...] = cnt_in_ref[...]

    x = x_ref[...]
    xn = x * lax.rsqrt(jnp.mean(x * x, axis=-1, keepdims=True) + NORM_EPS) * nmix_ref[...]
    gates = jax.nn.sigmoid(jnp.dot(xn.astype(BF16), wg_ref[...], preferred_element_type=F32)
                           + bg_ref[...])
    bf = jnp.dot(four_ref[...], wf_ref[...], preferred_element_type=F32)
    ba = jnp.dot(attn_ref[...], wa_ref[...], preferred_element_type=F32)
    merged = gates[:, :D_MODEL] * bf + gates[:, D_MODEL:] * ba
    h = x + jnp.dot(merged.astype(BF16), wo_ref[...], preferred_element_type=F32)
    h_ref[...] = h
    hn = h * lax.rsqrt(jnp.mean(h * h, axis=-1, keepdims=True) + NORM_EPS) * nffn_ref[...]
    hnp_ref[...] = _pack_bf16_pairs(hn[:, :D_MODEL // 2], hn[:, D_MODEL // 2:])

    lg = jnp.dot(hn, wr_ref[...], preferred_element_type=F32, precision=lax.Precision.HIGHEST)
    tm = lg.shape[0]
    lane = lax.broadcasted_iota(I32, (tm, ROUTER_LANES), 1)
    gmask = lane < N_GROUPS
    gl = jnp.where(gmask, lg, NEG_BIG)
    gmax = jnp.max(gl, axis=1, keepdims=True)
    g_idx = jnp.min(jnp.where(gl == gmax, lane, ROUTER_LANES), axis=1, keepdims=True)
    g_w = 1.0 / jnp.sum(jnp.where(gmask, jnp.exp(gl - gmax), 0.0), axis=1, keepdims=True)
    e_lane = lane - EXPERT_LANE0
    emask = (e_lane >= 0) & (e_lane < N_EXPERTS) & ((e_lane // EXPERTS_PER_GROUP) == g_idx)
    el = jnp.where(emask, lg, NEG_BIG)
    v1 = jnp.max(el, axis=1, keepdims=True)
    i1 = jnp.min(jnp.where(el == v1, lane, ROUTER_LANES), axis=1, keepdims=True)
    el2 = jnp.where(lane == i1, NEG_BIG, el)
    v2 = jnp.max(el2, axis=1, keepdims=True)
    i2 = jnp.min(jnp.where(el2 == v2, lane, ROUTER_LANES), axis=1, keepdims=True)
    s21 = jnp.exp(v2 - v1)
    w1 = g_w / (1.0 + s21)
    w2 = g_w * s21 / (1.0 + s21)

    oh1 = lane == i1
    oh2 = lane == i2
    ohs = (oh1 | oh2).astype(BF16)
    row = lax.broadcasted_iota(I32, (tm, tm), 0)
    col = lax.broadcasted_iota(I32, (tm, tm), 1)
    tril = (col < row).astype(BF16)
    before = jnp.dot(tril, ohs, preferred_element_type=F32) + carry_sc[...]
    r1 = jnp.sum(jnp.where(oh1, before, 0.0), axis=1, keepdims=True)
    r2 = jnp.sum(jnp.where(oh2, before, 0.0), axis=1, keepdims=True)
    carry_sc[...] = carry_sc[...] + jnp.sum(ohs.astype(F32), axis=0, keepdims=True)
    cnt_ref[...] = carry_sc[...]

    e1 = i1 - EXPERT_LANE0
    e2 = i2 - EXPERT_LANE0
    ri = jnp.where(lane == 0, e1, jnp.where(lane == 1, e2,
         jnp.where(lane == 2, r1.astype(I32), jnp.where(lane == 3, r2.astype(I32), 0))))
    ri_ref[...] = ri
    rw_ref[...] = jnp.where(lane == 0, w1, jnp.where(lane == 1, w2, 0.0))


def _merge(x, four, attn, nmix, wg, bg, wf, wa, wo, nffn, wr, cnt_in, tm):
    te = x.shape[0]
    nt = te // tm
    const = lambda i: (0, 0)
    tile = lambda i: (i, 0)
    return pl.pallas_call(
        _merge_kernel,
        grid=(nt,),
        in_specs=[
            pl.BlockSpec((tm, D_MODEL), tile),
            pl.BlockSpec((tm, FOURIER_WIDTH), tile),
            pl.BlockSpec((tm, V_WIDTH), tile),
            pl.BlockSpec((1, D_MODEL), const),
            pl.BlockSpec((D_MODEL, 2 * D_MODEL), const),
            pl.BlockSpec((1, 2 * D_MODEL), const),
            pl.BlockSpec((FOURIER_WIDTH, D_MODEL), const),
            pl.BlockSpec((V_WIDTH, D_MODEL), const),
            pl.BlockSpec((D_MODEL, D_MODEL), const),
            pl.BlockSpec((1, D_MODEL), const),
            pl.BlockSpec((D_MODEL, ROUTER_LANES), const),
            pl.BlockSpec((1, ROUTER_LANES), const),
        ],
        out_specs=[
            pl.BlockSpec((tm, D_MODEL), tile),
            pl.BlockSpec((tm, D_MODEL // 2), tile),
            pl.BlockSpec((tm, ROUTER_LANES), tile),
            pl.BlockSpec((tm, ROUTER_LANES), tile),
            pl.BlockSpec((1, ROUTER_LANES), const),
        ],
        out_shape=[
            jax.ShapeDtypeStruct((te, D_MODEL), F32),
            jax.ShapeDtypeStruct((te, D_MODEL // 2), U32),
            jax.ShapeDtypeStruct((te, ROUTER_LANES), I32),
            jax.ShapeDtypeStruct((te, ROUTER_LANES), F32),
            jax.ShapeDtypeStruct((1, ROUTER_LANES), F32),
        ],
        scratch_shapes=[pltpu.VMEM((1, ROUTER_LANES), F32)],
        compiler_params=_cparams(("arbitrary",)),
        name="merge_router",
    )(x, four, attn, nmix, wg, bg, wf, wa, wo, nffn, wr, cnt_in)


def _dispatch_kernel(zs_ref, zl_ref, dest_ref, *rest, tm, tiles):
    n_src = len(tiles)
    src_refs = rest[:n_src]
    xs_ref, zero_sc, sem, zsem = rest[n_src:]
    i = pl.program_id(0)

    def copy_tile(src_ref, base):
        def row_copy(t, slot):
            return pltpu.make_async_copy(src_ref.at[pl.ds(base + t, 1)],
                                         xs_ref.at[pl.ds(dest_ref[0, 0, 2 * t + slot], 1)], sem)

        def start(t, c):
            row_copy(t, 0).start()
            row_copy(t, 1).start()
            return c

        lax.fori_loop(0, tm, start, 0)

        def wait(t, c):
            row_copy(t, 0).wait()
            row_copy(t, 1).wait()
            return c

        lax.fori_loop(0, tm, wait, 0)

    first_tile = 0
    for src_ref, n_tiles in zip(src_refs, tiles):
        @pl.when((i >= first_tile) & (i < first_tile + n_tiles))
        def _(src_ref=src_ref, first_tile=first_tile):
            copy_tile(src_ref, (i - first_tile) * tm)
        first_tile += n_tiles

    @pl.when(i == 0)
    def _():
        zero_sc[...] = jnp.zeros(zero_sc.shape, U32)

        def zero_copy(r):
            return pltpu.make_async_copy(zero_sc.at[pl.ds(0, 1)], xs_ref.at[pl.ds(r, 1)], zsem)

        def per_expert(e, c):
            def zstart(r, c2):
                zero_copy(zs_ref[e] + r).start()
                return c2
            lax.fori_loop(0, zl_ref[e], zstart, 0)

            def zwait(r, c2):
                zero_copy(zs_ref[e] + r).wait()
                return c2
            lax.fori_loop(0, zl_ref[e], zwait, 0)
            return c
        lax.fori_loop(0, N_EXPERTS, per_expert, 0)


def _dispatch(zstart, zlen, dests, srcs, n_rows, tm):
    tiles = tuple(s.shape[0] // tm for s in srcs)
    dest3 = jnp.concatenate([d.reshape(n, 1, 2 * tm) for d, n in zip(dests, tiles)], axis=0)
    grid_spec = pltpu.PrefetchScalarGridSpec(
        num_scalar_prefetch=2,
        grid=(sum(tiles),),
        in_specs=[pl.BlockSpec((1, 1, 2 * tm), lambda i, zs, zl: (i, 0, 0), memory_space=pltpu.SMEM)]
        + [pl.BlockSpec(memory_space=pl.ANY)] * len(srcs),
        out_specs=pl.BlockSpec(memory_space=pl.ANY),
        scratch_shapes=[pltpu.VMEM((8, D_MODEL // 2), U32), pltpu.SemaphoreType.DMA,
                        pltpu.SemaphoreType.DMA],
    )
    return pl.pallas_call(
        functools.partial(_dispatch_kernel, tm=tm, tiles=tiles),
        grid_spec=grid_spec,
        out_shape=jax.ShapeDtypeStruct((n_rows, D_MODEL // 2), U32),
        compiler_params=_cparams(("arbitrary",)),
        name="moe_dispatch",
    )(zstart, zlen, dest3, *srcs)


def _expert_kernel(be_ref, nu_ref, xs_ref, wg_ref, wu_ref, wd_ref, o_ref, wgb, wub, wdb):
    b = pl.program_id(0)
    used = b < nu_ref[0]
    prev = be_ref[jnp.maximum(b - 1, 0)]
    new_expert = (b == 0) | (be_ref[b] != prev)

    @pl.when(used & new_expert)
    def _():
        rows = 128

        def cast(r, c):
            r0 = pl.multiple_of(r * rows, rows)
            wgb[pl.ds(r0, rows), :] = wg_ref[0, pl.ds(r0, rows), :].astype(BF16)
            wub[pl.ds(r0, rows), :] = wu_ref[0, pl.ds(r0, rows), :].astype(BF16)
            wdb[pl.ds(r0, rows), :] = wd_ref[0, pl.ds(r0, rows), :].astype(BF16)
            return c
        lax.fori_loop(0, D_MODEL // rows, cast, 0)

    @pl.when(used)
    def _():
        lo, hi = _unpack_bf16_pairs(xs_ref[...])
        x = jnp.concatenate([lo, hi], axis=1).astype(BF16)
        g = jnp.dot(x, wgb[...], preferred_element_type=F32)
        u = jnp.dot(x, wub[...], preferred_element_type=F32)
        hmid = (g * jax.nn.sigmoid(g) * u).astype(BF16)
        y = jnp.dot(hmid, wdb[...], preferred_element_type=F32)
        o_ref[...] = _pack_bf16_pairs(y[:, :D_MODEL // 2], y[:, D_MODEL // 2:])

    @pl.when(jnp.logical_not(used))
    def _():
        o_ref[...] = jnp.zeros(o_ref.shape, U32)


def _experts(block_expert, n_used, xs, wg, wu, wd):
    n_blocks = xs.shape[0] // MOE_ROWS
    wspec = pl.BlockSpec((1, D_MODEL, EXPERT_FF), lambda b, be, nu: (be[b], 0, 0))
    grid_spec = pltpu.PrefetchScalarGridSpec(
        num_scalar_prefetch=2,
        grid=(n_blocks,),
        in_specs=[
            pl.BlockSpec((MOE_ROWS, D_MODEL // 2), lambda b, be, nu: (b, 0)),
            wspec, wspec,
            pl.BlockSpec((1, EXPERT_FF, D_MODEL), lambda b, be, nu: (be[b], 0, 0)),
        ],
        out_specs=pl.BlockSpec((MOE_ROWS, D_MODEL // 2), lambda b, be, nu: (b, 0)),
        scratch_shapes=[pltpu.VMEM((D_MODEL, EXPERT_FF), BF16), pltpu.VMEM((D_MODEL, EXPERT_FF), BF16),
                        pltpu.VMEM((EXPERT_FF, D_MODEL), BF16)],
    )
    return pl.pallas_call(
        _expert_kernel,
        grid_spec=grid_spec,
        out_shape=jax.ShapeDtypeStruct((xs.shape[0], D_MODEL // 2), U32),
        compiler_params=_cparams(("arbitrary",)),
        name="moe_experts",
    )(block_expert, n_used, xs, wg, wu, wd)


def _combine_kernel(dest_ref, ys_ref, h_ref, rw_ref, nw_ref, o_ref, buf, sem, *, tm):
    def row_copy(t, slot):
        return pltpu.make_async_copy(ys_ref.at[pl.ds(dest_ref[0, 0, 2 * t + slot], 1)],
                                     buf.at[slot, pl.ds(t, 1)], sem)

    def start(t, c):
        row_copy(t, 0).start()
        row_copy(t, 1).start()
        return c

    lax.fori_loop(0, tm, start, 0)

    def wait(t, c):
        row_copy(t, 0).wait()
        row_copy(t, 1).wait()
        return c

    lax.fori_loop(0, tm, wait, 0)

    rw = rw_ref[...]
    w1 = rw[:, 0:1]
    w2 = rw[:, 1:2]
    lo1, hi1 = _unpack_bf16_pairs(buf[0])
    lo2, hi2 = _unpack_bf16_pairs(buf[1])
    y = jnp.concatenate([lo1 * w1 + lo2 * w2, hi1 * w1 + hi2 * w2], axis=1)
    h = h_ref[...] + y
    o_ref[...] = h * lax.rsqrt(jnp.mean(h * h, axis=-1, keepdims=True) + NORM_EPS) * nw_ref[...]


def _combine(dest, ys, h, rw, nw, tm):
    te = h.shape[0]
    nt = te // tm
    dest3 = dest.reshape(nt, 1, 2 * tm)
    return pl.pallas_call(
        functools.partial(_combine_kernel, tm=tm),
        grid=(nt,),
        in_specs=[
            pl.BlockSpec((1, 1, 2 * tm), lambda i: (i, 0, 0), memory_space=pltpu.SMEM),
            pl.BlockSpec(memory_space=pl.ANY),
            pl.BlockSpec((tm, D_MODEL), lambda i: (i, 0)),
            pl.BlockSpec((tm, ROUTER_LANES), lambda i: (i, 0)),
            pl.BlockSpec((1, D_MODEL), lambda i: (0, 0)),
        ],
        out_specs=pl.BlockSpec((tm, D_MODEL), lambda i: (i, 0)),
        out_shape=jax.ShapeDtypeStruct((te, D_MODEL), F32),
        scratch_shapes=[pltpu.VMEM((2, tm, D_MODEL // 2), U32), pltpu.SemaphoreType.DMA],
        compiler_params=_cparams(("arbitrary",)),
        name="moe_combine",
    )(dest3, ys, h, rw, nw)


def _rope_tables(batch, seq):
    pos = jnp.arange(seq, dtype=F32)
    inv_freq = ROPE_THETA ** (-jnp.arange(0, HEAD_DIM, 2, dtype=F32) / HEAD_DIM)
    ang = pos[:, None] * inv_freq[None, :]
    c, s = jnp.cos(ang), jnp.sin(ang)
    cos = jnp.concatenate([c, c, c, c], axis=1)
    sin = jnp.concatenate([-s, s, -s, s], axis=1)
    return jnp.tile(cos, (batch, 1)), jnp.tile(sin, (batch, 1))


def _token_mixer(x2d, batch, seq, p, cnt_in):
    cos, sin = _rope_tables(batch, seq)
    f, qz, k, vt = _inproj(x2d, p["nmix"], p["w1"], cos, sin, tm=512)
    four = _fourier(f, batch, seq)
    attn = _attention(qz, k, vt, p["lam"], p["sub"], batch, seq, tq=512, ck=512,
                      kv_block=min(seq, 2048))
    return _merge(x2d, four, attn, p["nmix"], p["wgate"], p["bg"], p["wf"], p["wa"], p["wo"],
                  p["nffn"], p["wr"], cnt_in, tm=256)


def kernel(x_prompt, x_sample, norm_mix_w, w_in, b_gate, w_fourier, lambda_q1, lambda_k1, lambda_q2,
           lambda_k2, subln_w, w_attn, w_out, norm_ffn_w, w_group_router, w_expert_router,
           w_expert_gate, w_expert_up, w_expert_down, norm_final_w):
    wr = jnp.concatenate([w_group_router[0], w_expert_router[0]], axis=1)
    wr = jnp.pad(wr, ((0, 0), (0, ROUTER_LANES - wr.shape[1])))
    p = {
        "nmix": norm_mix_w[0][None, :],
        "w1": w_in[0][:, :2048].astype(BF16),
        "wgate": w_in[0][:, 2048:].astype(BF16),
        "bg": b_gate[0][None, :],
        "wf": w_fourier[0].astype(BF16),
        "wa": w_attn[0].astype(BF16),
        "wo": w_out[0].astype(BF16),
        "nffn": norm_ffn_w[0][None, :],
        "wr": wr,
        "lam": jnp.stack([lambda_q1[0], lambda_k1[0], lambda_q2[0], lambda_k2[0]]),
        "sub": subln_w[0][:, None],
    }
    inputs = [x_prompt, x_sample]
    mixed = []
    cnt = jnp.zeros((1, ROUTER_LANES), F32)
    for x in inputs:
        batch, seq, _ = x.shape
        h, hnp, ri, rw, cnt = _token_mixer(x.reshape(batch * seq, D_MODEL), batch, seq, p, cnt)
        mixed.append((h, hnp, ri, rw))
    outs = _moe_and_final(mixed, cnt, w_expert_gate[0], w_expert_up[0], w_expert_down[0], norm_final_w)
    return tuple(o.reshape(x.shape) for o, x in zip(outs, inputs))


def _moe_and_final(mixed, cnt, w_gate, w_up, w_down, norm_final_w):
    n_assign = 2 * sum(m[0].shape[0] for m in mixed)
    n_blocks = -(-(n_assign + N_EXPERTS * (MOE_ROWS - 1)) // MOE_ROWS)
    n_rows = n_blocks * MOE_ROWS
    counts = cnt[0, EXPERT_LANE0:EXPERT_LANE0 + N_EXPERTS].astype(I32)
    padded = (counts + MOE_ROWS - 1) // MOE_ROWS * MOE_ROWS
    pend = jnp.cumsum(padded)
    pstart = pend - padded
    total = pend[-1]
    bstart = jnp.arange(n_blocks, dtype=I32) * MOE_ROWS
    be = jnp.minimum(jnp.searchsorted(pend, bstart, side="right"), N_EXPERTS - 1).astype(I32)
    n_used = (total // MOE_ROWS).astype(I32)
    be = jnp.where(bstart < total, be, be[jnp.maximum(n_used - 1, 0)])
    zstart = pstart + counts
    zend = jnp.where(jnp.arange(N_EXPERTS) == N_EXPERTS - 1, n_rows, pend)
    zlen = (zend - zstart).astype(I32)

    dests = [pstart[m[2][:, 0:2]] + m[2][:, 2:4] for m in mixed]
    xs = _dispatch(zstart, zlen, dests, [m[1] for m in mixed], n_rows, tm=256)
    ys = _experts(be, n_used[None], xs, w_gate, w_up, w_down)
    return [_combine(dest, ys, m[0], m[3], norm_final_w[None, :], tm=256)
            for m, dest in zip(mixed, dests)]
```

```python
import functools
import math

import numpy as np
import jax
import jax.numpy as jnp
from jax import lax
from jax.experimental import pallas as pl
from jax.experimental.pallas import tpu as pltpu

F32 = jnp.float32
BF16 = jnp.bfloat16
U32 = jnp.uint32
I32 = jnp.int32

D_MODEL = 1024
FOURIER_WIDTH = 512
FOURIER_GROUPS = 4
GROUP_DIM = 128
HEADS = 4
HEAD_DIM = 64
QK_WIDTH = 512
V_DIM = 128
V_WIDTH = 512
ROPE_THETA = 10000.0
N_GROUPS = 4
EXPERTS_PER_GROUP = 8
N_EXPERTS = 32
EXPERT_FF = 1024
NORM_EPS = 1e-6
LAMBDA_INIT = 0.8 - 0.6 * math.exp(0.0)

LANES = 128
ROUTER_LANES = 128
EXPERT_LANE0 = N_GROUPS
MOE_ROWS = 256
DFT_N = 128
VMEM_LIMIT = 56 * 1024 * 1024

LOG2E = 1.4426950408889634
NEG_BIG = -3.0e38


def _cparams(sem, vmem=VMEM_LIMIT):
    return pltpu.CompilerParams(dimension_semantics=sem, vmem_limit_bytes=vmem)


def _swap_halves(t):
    n = t.shape[1]
    lane = lax.broadcasted_iota(I32, t.shape, 1)
    first = (lane % HEAD_DIM) < (HEAD_DIM // 2)
    return jnp.where(first, pltpu.roll(t, n - HEAD_DIM // 2, 1), pltpu.roll(t, HEAD_DIM // 2, 1))


def _inproj_kernel(x_ref, nw_ref, w_ref, cos_ref, sin_ref, f_ref, qz_ref, k_ref, vt_ref):
    x = x_ref[...]
    xn = x * lax.rsqrt(jnp.mean(x * x, axis=-1, keepdims=True) + NORM_EPS) * nw_ref[...]
    proj = jnp.dot(xn.astype(BF16), w_ref[...], preferred_element_type=F32)
    f_ref[...] = proj[:, :FOURIER_WIDTH].astype(BF16)
    q = proj[:, FOURIER_WIDTH:FOURIER_WIDTH + QK_WIDTH]
    k = proj[:, FOURIER_WIDTH + QK_WIDTH:FOURIER_WIDTH + 2 * QK_WIDTH]
    v = proj[:, FOURIER_WIDTH + 2 * QK_WIDTH:]
    cos = jnp.concatenate([cos_ref[...]] * (QK_WIDTH // LANES), axis=1)
    sin = jnp.concatenate([sin_ref[...]] * (QK_WIDTH // LANES), axis=1)
    q = q * cos + _swap_halves(q) * sin
    k = k * cos + _swap_halves(k) * sin
    q = q * (HEAD_DIM ** -0.5 * LOG2E)
    lane = lax.broadcasted_iota(I32, (q.shape[0], LANES), 1)
    lo = lane < HEAD_DIM
    vt = v.T.astype(BF16)
    for h in range(HEADS):
        qh = q[:, h * LANES:(h + 1) * LANES]
        qz_ref[2 * h] = jnp.where(lo, qh, 0.0).astype(BF16)
        qz_ref[2 * h + 1] = jnp.where(lo, 0.0, qh).astype(BF16)
        k_ref[h] = k[:, h * LANES:(h + 1) * LANES].astype(BF16)
        vt_ref[0, h] = vt[h * V_DIM:(h + 1) * V_DIM, :]


def _inproj(x, nw, w1, cos, sin, tm):
    te = x.shape[0]
    nt = te // tm
    return pl.pallas_call(
        _inproj_kernel,
        grid=(nt,),
        in_specs=[
            pl.BlockSpec((tm, D_MODEL), lambda i: (i, 0)),
            pl.BlockSpec((1, D_MODEL), lambda i: (0, 0)),
            pl.BlockSpec((D_MODEL, 2048), lambda i: (0, 0)),
            pl.BlockSpec((tm, LANES), lambda i: (i, 0)),
            pl.BlockSpec((tm, LANES), lambda i: (i, 0)),
        ],
        out_specs=[
            pl.BlockSpec((tm, FOURIER_WIDTH), lambda i: (i, 0)),
            pl.BlockSpec((2 * HEADS, tm, LANES), lambda i: (0, i, 0)),
            pl.BlockSpec((HEADS, tm, LANES), lambda i: (0, i, 0)),
            pl.BlockSpec((1, HEADS, V_DIM, tm), lambda i: (i, 0, 0, 0)),
        ],
        out_shape=[
            jax.ShapeDtypeStruct((te, FOURIER_WIDTH), BF16),
            jax.ShapeDtypeStruct((2 * HEADS, te, LANES), BF16),
            jax.ShapeDtypeStruct((HEADS, te, LANES), BF16),
            jax.ShapeDtypeStruct((nt, HEADS, V_DIM, tm), BF16),
        ],
        compiler_params=_cparams(("parallel",)),
        name="inproj",
    )(x, nw, w1, cos, sin)


@functools.lru_cache(maxsize=None)
def _dft_tables(batch, seq):
    n1 = seq // DFT_N
    assert batch * n1 == DFT_N
    idx = np.arange(n1)
    ang = 2.0 * np.pi * ((idx[:, None] * idx[None, :]) % n1) / n1
    eye = np.eye(batch)
    m1 = np.concatenate([np.kron(eye, np.cos(ang)), -np.kron(eye, np.sin(ang))], axis=0)
    k1 = np.arange(n1)[:, None, None]
    k2 = np.arange(DFT_N)[None, :, None]
    n2 = np.arange(DFT_N)[None, None, :]
    ang2 = 2.0 * np.pi * ((n2 * (k1 + n1 * k2)) % seq) / seq
    er, ei = np.cos(ang2), -np.sin(ang2)
    m2 = np.concatenate([np.concatenate([er, -ei], axis=2), np.concatenate([ei, er], axis=2)], axis=1)
    c = np.arange(GROUP_DIM)
    ang3 = 2.0 * np.pi * ((c[:, None] * c[None, :]) % GROUP_DIM) / GROUP_DIM
    norm = 1.0 / math.sqrt(seq * GROUP_DIM)
    cs = np.concatenate([np.cos(ang3), np.sin(ang3)], axis=0) * norm
    return (jnp.asarray(m1, BF16), jnp.asarray(m2, BF16), jnp.asarray(cs, BF16))


def _dft1_kernel(m_ref, x_ref, y_ref):
    y_ref[...] = jnp.dot(m_ref[...], x_ref[...], preferred_element_type=F32).astype(BF16)


def _dft1(m1, x2d, tn=4096):
    n = x2d.shape[1]
    return pl.pallas_call(
        _dft1_kernel,
        grid=(n // tn,),
        in_specs=[pl.BlockSpec((2 * DFT_N, DFT_N), lambda i: (0, 0)),
                  pl.BlockSpec((DFT_N, tn), lambda i: (0, i))],
        out_specs=pl.BlockSpec((2 * DFT_N, tn), lambda i: (0, i)),
        out_shape=jax.ShapeDtypeStruct((2 * DFT_N, n), BF16),
        compiler_params=_cparams(("parallel",)),
        name="dft_stage1",
    )(m1, x2d)


def _dft2_kernel(m_ref, yr_ref, yi_ref, cs_ref, o_ref, *, rows_per_step):
    for j in range(rows_per_step):
        m = m_ref[j]
        z = (jnp.dot(m[:, :DFT_N], yr_ref[0, j], preferred_element_type=F32)
             + jnp.dot(m[:, DFT_N:], yi_ref[0, j], preferred_element_type=F32))
        zb = z.astype(BF16)
        for g in range(FOURIER_GROUPS):
            zg = jnp.concatenate([zb[:DFT_N, g * LANES:(g + 1) * LANES],
                                  zb[DFT_N:, g * LANES:(g + 1) * LANES]], axis=1)
            o_ref[0, :, (j * FOURIER_GROUPS + g) * LANES:(j * FOURIER_GROUPS + g + 1) * LANES] = (
                jnp.dot(zg, cs_ref[...], preferred_element_type=F32).astype(BF16))


def _dft2(m2, y, cs, batch, seq, rows_per_step=4):
    n1 = seq // DFT_N
    r = rows_per_step
    steps = DFT_N // r
    per_b = n1 // r
    return pl.pallas_call(
        functools.partial(_dft2_kernel, rows_per_step=r),
        grid=(steps,),
        in_specs=[
            pl.BlockSpec((r, 2 * DFT_N, 2 * DFT_N), lambda i: (i % per_b, 0, 0)),
            pl.BlockSpec((1, r, DFT_N, FOURIER_WIDTH), lambda i: (0, i, 0, 0)),
            pl.BlockSpec((1, r, DFT_N, FOURIER_WIDTH), lambda i: (1, i, 0, 0)),
            pl.BlockSpec((2 * GROUP_DIM, GROUP_DIM), lambda i: (0, 0)),
        ],
        out_specs=pl.BlockSpec((1, DFT_N, r * FOURIER_WIDTH), lambda i: (i // per_b, 0, i % per_b)),
        out_shape=jax.ShapeDtypeStruct((batch, DFT_N, n1 * FOURIER_WIDTH), BF16),
        compiler_params=_cparams(("parallel",)),
        name="dft_stage2",
    )(m2, y, y, cs)


def _fourier(f, batch, seq):
    m1, m2, cs = _dft_tables(batch, seq)
    y = _dft1(m1, f.reshape(DFT_N, DFT_N * FOURIER_WIDTH))
    out = _dft2(m2, y.reshape(2, DFT_N, DFT_N, FOURIER_WIDTH), cs, batch, seq)
    return out.reshape(batch * seq, FOURIER_WIDTH)


def _attn_kernel(qz_ref, k_ref, vt_ref, lam_ref, sub_ref, o_ref,
                 s_a, s_b, mx_a, mx_b, m_a, m_b, l_a, l_b, acc_a, acc_b, *, ck, n_chunks):
    j = pl.program_id(2)
    n_items = n_chunks * HEADS

    @pl.when(j == 0)
    def _():
        for m_sc, l_sc, acc_sc in ((m_a, l_a, acc_a), (m_b, l_b, acc_b)):
            m_sc[...] = jnp.full(m_sc.shape, NEG_BIG, F32)
            l_sc[...] = jnp.zeros(l_sc.shape, F32)
            acc_sc[...] = jnp.zeros(acc_sc.shape, F32)

    def split(t):
        t = jnp.asarray(t, I32)
        return lax.shift_right_logical(t, 2), t & (HEADS - 1)

    def scores(t, cc, s_ref, mx_ref):
        c, h = split(t)
        row0 = pl.multiple_of(c * ck, ck)
        s = lax.dot_general(k_ref[h, pl.ds(row0, ck), :], qz_ref[2 * h + cc],
                            (((1,), (1,)), ((), ())), preferred_element_type=F32)
        s_ref[...] = s
        mx_ref[...] = jnp.max(s, axis=0, keepdims=True)

    def update(t, s_ref, mx_ref, m_sc, l_sc, acc_sc):
        c, h = split(t)
        m_old = m_sc[h]
        m_new = jnp.maximum(m_old, mx_ref[...])
        p = jnp.exp2(s_ref[...] - m_new)
        alpha = jnp.exp2(m_old - m_new)
        l_sc[h] = alpha * l_sc[h] + jnp.sum(p, axis=0, keepdims=True)
        pb = p.astype(BF16)
        ckv = vt_ref.shape[-1]
        pv = None
        for u in range(ck // ckv):
            d = jnp.dot(vt_ref[c * (ck // ckv) + u, h], pb[u * ckv:(u + 1) * ckv],
                        preferred_element_type=F32)
            pv = d if pv is None else pv + d
        acc_sc[h] = alpha * acc_sc[h] + pv
        m_sc[h] = m_new

    scores(0, 0, s_a, mx_a)
    scores(0, 1, s_b, mx_b)

    def body(t, carry):
        update(t, s_a, mx_a, m_a, l_a, acc_a)
        scores(t + 1, 0, s_a, mx_a)
        update(t, s_b, mx_b, m_b, l_b, acc_b)
        scores(t + 1, 1, s_b, mx_b)
        return carry

    lax.fori_loop(0, n_items - 1, body, 0)
    update(n_items - 1, s_a, mx_a, m_a, l_a, acc_a)
    update(n_items - 1, s_b, mx_b, m_b, l_b, acc_b)

    @pl.when(j == pl.num_programs(2) - 1)
    def _():
        lv = lam_ref[...]
        lam = (jnp.exp(jnp.sum(lv[0:1] * lv[1:2], axis=1, keepdims=True))
               - jnp.exp(jnp.sum(lv[2:3] * lv[3:4], axis=1, keepdims=True)) + LAMBDA_INIT)
        for h in range(HEADS):
            o = acc_a[h] / l_a[h] - lam * (acc_b[h] / l_b[h])
            o = o * lax.rsqrt(jnp.mean(o * o, axis=0, keepdims=True) + NORM_EPS)
            o = o * sub_ref[...] * (1.0 - LAMBDA_INIT)
            o_ref[:, h * V_DIM:(h + 1) * V_DIM] = o.T.astype(BF16)


def _attention(qz, k, vt, lam_vecs, sub_col, batch, seq, tq, ck, kv_block):
    nq = seq // tq
    nkb = seq // kv_block
    n_chunks = kv_block // ck
    ckv = vt.shape[-1]
    stat = pltpu.VMEM((HEADS, 1, tq), F32)
    return pl.pallas_call(
        functools.partial(_attn_kernel, ck=ck, n_chunks=n_chunks),
        grid=(batch, nq, nkb),
        in_specs=[
            pl.BlockSpec((2 * HEADS, tq, LANES), lambda b, i, j: (0, b * nq + i, 0)),
            pl.BlockSpec((HEADS, kv_block, LANES), lambda b, i, j: (0, b * nkb + j, 0)),
            pl.BlockSpec((kv_block // ckv, HEADS, V_DIM, ckv), lambda b, i, j: (b * nkb + j, 0, 0, 0)),
            pl.BlockSpec((4, HEAD_DIM), lambda b, i, j: (0, 0)),
            pl.BlockSpec((V_DIM, 1), lambda b, i, j: (0, 0)),
        ],
        out_specs=pl.BlockSpec((tq, V_WIDTH), lambda b, i, j: (b * nq + i, 0)),
        out_shape=jax.ShapeDtypeStruct((batch * seq, V_WIDTH), BF16),
        scratch_shapes=[
            pltpu.VMEM((ck, tq), F32), pltpu.VMEM((ck, tq), F32),
            pltpu.VMEM((1, tq), F32), pltpu.VMEM((1, tq), F32),
            stat, stat, stat, stat,
            pltpu.VMEM((HEADS, V_DIM, tq), F32), pltpu.VMEM((HEADS, V_DIM, tq), F32),
        ],
        compiler_params=_cparams(("parallel", "parallel", "arbitrary")),
        name="diff_attention",
    )(qz, k, vt, lam_vecs, sub_col)


def _pack_bf16_pairs(a, b):
    ab = lax.bitcast_convert_type(a.astype(BF16).astype(F32), U32)
    bb = lax.bitcast_convert_type(b.astype(BF16).astype(F32), U32)
    return (bb & jnp.uint32(0xFFFF0000)) | (ab >> 16)


def _unpack_bf16_pairs(w):
    lo = lax.bitcast_convert_type(w << 16, F32)
    hi = lax.bitcast_convert_type(w & jnp.uint32(0xFFFF0000), F32)
    return lo, hi


def _merge_kernel(x_ref, four_ref, attn_ref, nmix_ref, wg_ref, bg_ref, wf_ref, wa_ref, wo_ref,
                  nffn_ref, wr_ref, cnt_in_ref,
                  h_ref, hnp_ref, ri_ref, rw_ref, cnt_ref, carry_sc):
    i = pl.program_id(0)

    @pl.when(i == 0)
    def _():
        carry_sc[...] = cnt_in_ref[...]

    x = x_ref[...]
    xn = x * lax.rsqrt(jnp.mean(x * x, axis=-1, keepdims=True) + NORM_EPS) * nmix_ref[...]
    gates = jax.nn.sigmoid(jnp.dot(xn.astype(BF16), wg_ref[...], preferred_element_type=F32)
                           + bg_ref[...])
    bf = jnp.dot(four_ref[...], wf_ref[...], preferred_element_type=F32)
    ba = jnp.dot(attn_ref[...], wa_ref[...], preferred_element_type=F32)
    merged = gates[:, :D_MODEL] * bf + gates[:, D_MODEL:] * ba
    h = x + jnp.dot(merged.astype(BF16), wo_ref[...], preferred_element_type=F32)
    h_ref[...] = h
    hn = h * lax.rsqrt(jnp.mean(h * h, axis=-1, keepdims=True) + NORM_EPS) * nffn_ref[...]
    hnp_ref[...] = _pack_bf16_pairs(hn[:, :D_MODEL // 2], hn[:, D_MODEL // 2:])

    lg = jnp.dot(hn, wr_ref[...], preferred_element_type=F32, precision=lax.Precision.HIGHEST)
    tm = lg.shape[0]
    lane = lax.broadcasted_iota(I32, (tm, ROUTER_LANES), 1)
    gmask = lane < N_GROUPS
    gl = jnp.where(gmask, lg, NEG_BIG)
    gmax = jnp.max(gl, axis=1, keepdims=True)
    g_idx = jnp.min(jnp.where(gl == gmax, lane, ROUTER_LANES), axis=1, keepdims=True)
    g_w = 1.0 / jnp.sum(jnp.where(gmask, jnp.exp(gl - gmax), 0.0), axis=1, keepdims=True)
    e_lane = lane - EXPERT_LANE0
    emask = (e_lane >= 0) & (e_lane < N_EXPERTS) & ((e_lane // EXPERTS_PER_GROUP) == g_idx)
    el = jnp.where(emask, lg, NEG_BIG)
    v1 = jnp.max(el, axis=1, keepdims=True)
    i1 = jnp.min(jnp.where(el == v1, lane, ROUTER_LANES), axis=1, keepdims=True)
    el2 = jnp.where(lane == i1, NEG_BIG, el)
    v2 = jnp.max(el2, axis=1, keepdims=True)
    i2 = jnp.min(jnp.where(el2 == v2, lane, ROUTER_LANES), axis=1, keepdims=True)
    s21 = jnp.exp(v2 - v1)
    w1 = g_w / (1.0 + s21)
    w2 = g_w * s21 / (1.0 + s21)

    oh1 = lane == i1
    oh2 = lane == i2
    ohs = (oh1 | oh2).astype(BF16)
    row = lax.broadcasted_iota(I32, (tm, tm), 0)
    col = lax.broadcasted_iota(I32, (tm, tm), 1)
    tril = (col < row).astype(BF16)
    before = jnp.dot(tril, ohs, preferred_element_type=F32) + carry_sc[...]
    r1 = jnp.sum(jnp.where(oh1, before, 0.0), axis=1, keepdims=True)
    r2 = jnp.sum(jnp.where(oh2, before, 0.0), axis=1, keepdims=True)
    carry_sc[...] = carry_sc[...] + jnp.sum(ohs.astype(F32), axis=0, keepdims=True)
    cnt_ref[...] = carry_sc[...]

    e1 = i1 - EXPERT_LANE0
    e2 = i2 - EXPERT_LANE0
    ri = jnp.where(lane == 0, e1, jnp.where(lane == 1, e2,
         jnp.where(lane == 2, r1.astype(I32), jnp.where(lane == 3, r2.astype(I32), 0))))
    ri_ref[...] = ri
    rw_ref[...] = jnp.where(lane == 0, w1, jnp.where(lane == 1, w2, 0.0))


def _merge(x, four, attn, nmix, wg, bg, wf, wa, wo, nffn, wr, cnt_in, tm):
    te = x.shape[0]
    nt = te // tm
    const = lambda i: (0, 0)
    tile = lambda i: (i, 0)
    return pl.pallas_call(
        _merge_kernel,
        grid=(nt,),
        in_specs=[
            pl.BlockSpec((tm, D_MODEL), tile),
            pl.BlockSpec((tm, FOURIER_WIDTH), tile),
            pl.BlockSpec((tm, V_WIDTH), tile),
            pl.BlockSpec((1, D_MODEL), const),
            pl.BlockSpec((D_MODEL, 2 * D_MODEL), const),
            pl.BlockSpec((1, 2 * D_MODEL), const),
            pl.BlockSpec((FOURIER_WIDTH, D_MODEL), const),
            pl.BlockSpec((V_WIDTH, D_MODEL), const),
            pl.BlockSpec((D_MODEL, D_MODEL), const),
            pl.BlockSpec((1, D_MODEL), const),
            pl.BlockSpec((D_MODEL, ROUTER_LANES), const),
            pl.BlockSpec((1, ROUTER_LANES), const),
        ],
        out_specs=[
            pl.BlockSpec((tm, D_MODEL), tile),
            pl.BlockSpec((tm, D_MODEL // 2), tile),
            pl.BlockSpec((tm, ROUTER_LANES), tile),
            pl.BlockSpec((tm, ROUTER_LANES), tile),
            pl.BlockSpec((1, ROUTER_LANES), const),
        ],
        out_shape=[
            jax.ShapeDtypeStruct((te, D_MODEL), F32),
            jax.ShapeDtypeStruct((te, D_MODEL // 2), U32),
            jax.ShapeDtypeStruct((te, ROUTER_LANES), I32),
            jax.ShapeDtypeStruct((te, ROUTER_LANES), F32),
            jax.ShapeDtypeStruct((1, ROUTER_LANES), F32),
        ],
        scratch_shapes=[pltpu.VMEM((1, ROUTER_LANES), F32)],
        compiler_params=_cparams(("arbitrary",)),
        name="merge_router",
    )(x, four, attn, nmix, wg, bg, wf, wa, wo, nffn, wr, cnt_in)


def _dispatch_kernel(zs_ref, zl_ref, dest_ref, *rest, tm, tiles):
    n_src = len(tiles)
    src_refs = rest[:n_src]
    xs_ref, zero_sc, sem, zsem = rest[n_src:]
    i = pl.program_id(0)

    def copy_tile(src_ref):
        def row_copy(t, slot):
            return pltpu.make_async_copy(src_ref.at[pl.ds(t, 1)],
                                         xs_ref.at[pl.ds(dest_ref[0, 0, 2 * t + slot], 1)], sem)

        def start(t, c):
            row_copy(t, 0).start()
            row_copy(t, 1).start()
            return c

        lax.fori_loop(0, tm, start, 0, unroll=8)

        def wait(t, c):
            row_copy(t, 0).wait()
            row_copy(t, 1).wait()
            return c

        lax.fori_loop(0, tm, wait, 0, unroll=8)

    first_tile = 0
    for src_ref, n_tiles in zip(src_refs, tiles):
        @pl.when((i >= first_tile) & (i < first_tile + n_tiles))
        def _(src_ref=src_ref):
            copy_tile(src_ref)
        first_tile += n_tiles

    @pl.when(i == 0)
    def _():
        zero_sc[...] = jnp.zeros(zero_sc.shape, U32)

        def zero_copy(r):
            return pltpu.make_async_copy(zero_sc.at[pl.ds(0, 1)], xs_ref.at[pl.ds(r, 1)], zsem)

        def per_expert(e, c):
            def zstart(r, c2):
                zero_copy(zs_ref[e] + r).start()
                return c2
            lax.fori_loop(0, zl_ref[e], zstart, 0)

            def zwait(r, c2):
                zero_copy(zs_ref[e] + r).wait()
                return c2
            lax.fori_loop(0, zl_ref[e], zwait, 0)
            return c
        lax.fori_loop(0, N_EXPERTS, per_expert, 0)


def _dispatch(zstart, zlen, dests, srcs, n_rows, tm):
    tiles = tuple(s.shape[0] // tm for s in srcs)
    dest3 = jnp.concatenate([d.reshape(n, 1, 2 * tm) for d, n in zip(dests, tiles)], axis=0)
    src_specs = []
    first_tile = 0
    for n in tiles:
        src_specs.append(pl.BlockSpec(
            (tm, D_MODEL // 2), lambda i, zs, zl, ft=first_tile, n=n: (jnp.clip(i - ft, 0, n - 1), 0)))
        first_tile += n
    grid_spec = pltpu.PrefetchScalarGridSpec(
        num_scalar_prefetch=2,
        grid=(sum(tiles),),
        in_specs=[pl.BlockSpec((1, 1, 2 * tm), lambda i, zs, zl: (i, 0, 0), memory_space=pltpu.SMEM)]
        + src_specs,
        out_specs=pl.BlockSpec(memory_space=pl.ANY),
        scratch_shapes=[pltpu.VMEM((8, D_MODEL // 2), U32), pltpu.SemaphoreType.DMA,
                        pltpu.SemaphoreType.DMA],
    )
    return pl.pallas_call(
        functools.partial(_dispatch_kernel, tm=tm, tiles=tiles),
        grid_spec=grid_spec,
        out_shape=jax.ShapeDtypeStruct((n_rows, D_MODEL // 2), U32),
        compiler_params=_cparams(("arbitrary",)),
        name="moe_dispatch",
    )(zstart, zlen, dest3, *srcs)


def _expert_kernel(be_ref, nu_ref, xs_ref, wg_ref, wu_ref, wd_ref, o_ref, wgb, wub, wdb):
    b = pl.program_id(0)
    used = b < nu_ref[0]
    prev = be_ref[jnp.maximum(b - 1, 0)]
    new_expert = (b == 0) | (be_ref[b] != prev)

    @pl.when(used & new_expert)
    def _():
        rows = 128

        def cast(r, c):
            r0 = pl.multiple_of(r * rows, rows)
            wgb[pl.ds(r0, rows), :] = wg_ref[0, pl.ds(r0, rows), :].astype(BF16)
            wub[pl.ds(r0, rows), :] = wu_ref[0, pl.ds(r0, rows), :].astype(BF16)
            wdb[pl.ds(r0, rows), :] = wd_ref[0, pl.ds(r0, rows), :].astype(BF16)
            return c
        lax.fori_loop(0, D_MODEL // rows, cast, 0)

    @pl.when(used)
    def _():
        lo, hi = _unpack_bf16_pairs(xs_ref[...])
        x = jnp.concatenate([lo, hi], axis=1).astype(BF16)
        g = jnp.dot(x, wgb[...], preferred_element_type=F32)
        u = jnp.dot(x, wub[...], preferred_element_type=F32)
        hmid = (g * jax.nn.sigmoid(g) * u).astype(BF16)
        y = jnp.dot(hmid, wdb[...], preferred_element_type=F32)
        o_ref[...] = _pack_bf16_pairs(y[:, :D_MODEL // 2], y[:, D_MODEL // 2:])

    @pl.when(jnp.logical_not(used))
    def _():
        o_ref[...] = jnp.zeros(o_ref.shape, U32)


def _experts(block_expert, n_used, xs, wg, wu, wd):
    n_blocks = xs.shape[0] // MOE_ROWS
    wspec = pl.BlockSpec((1, D_MODEL, EXPERT_FF), lambda b, be, nu: (be[b], 0, 0))
    grid_spec = pltpu.PrefetchScalarGridSpec(
        num_scalar_prefetch=2,
        grid=(n_blocks,),
        in_specs=[
            pl.BlockSpec((MOE_ROWS, D_MODEL // 2), lambda b, be, nu: (b, 0)),
            wspec, wspec,
            pl.BlockSpec((1, EXPERT_FF, D_MODEL), lambda b, be, nu: (be[b], 0, 0)),
        ],
        out_specs=pl.BlockSpec((MOE_ROWS, D_MODEL // 2), lambda b, be, nu: (b, 0)),
        scratch_shapes=[pltpu.VMEM((D_MODEL, EXPERT_FF), BF16), pltpu.VMEM((D_MODEL, EXPERT_FF), BF16),
                        pltpu.VMEM((EXPERT_FF, D_MODEL), BF16)],
    )
    return pl.pallas_call(
        _expert_kernel,
        grid_spec=grid_spec,
        out_shape=jax.ShapeDtypeStruct((xs.shape[0], D_MODEL // 2), U32),
        compiler_params=_cparams(("arbitrary",)),
        name="moe_experts",
    )(block_expert, n_used, xs, wg, wu, wd)


def _combine_kernel(dest_ref, ys_ref, h_ref, rw_ref, nw_ref, o_ref, buf, sem, *, tm):
    def row_copy(t, slot):
        return pltpu.make_async_copy(ys_ref.at[pl.ds(dest_ref[0, 0, 2 * t + slot], 1)],
                                     buf.at[slot, pl.ds(t, 1)], sem)

    def start(t, c):
        row_copy(t, 0).start()
        row_copy(t, 1).start()
        return c

    lax.fori_loop(0, tm, start, 0, unroll=8)

    def wait(t, c):
        row_copy(t, 0).wait()
        row_copy(t, 1).wait()
        return c

    lax.fori_loop(0, tm, wait, 0, unroll=8)

    rw = rw_ref[...]
    w1 = rw[:, 0:1]
    w2 = rw[:, 1:2]
    lo1, hi1 = _unpack_bf16_pairs(buf[0])
    lo2, hi2 = _unpack_bf16_pairs(buf[1])
    y = jnp.concatenate([lo1 * w1 + lo2 * w2, hi1 * w1 + hi2 * w2], axis=1)
    h = h_ref[...] + y
    o_ref[...] = h * lax.rsqrt(jnp.mean(h * h, axis=-1, keepdims=True) + NORM_EPS) * nw_ref[...]


def _combine(dest, ys, h, rw, nw, tm):
    te = h.shape[0]
    nt = te // tm
    dest3 = dest.reshape(nt, 1, 2 * tm)
    return pl.pallas_call(
        functools.partial(_combine_kernel, tm=tm),
        grid=(nt,),
        in_specs=[
            pl.BlockSpec((1, 1, 2 * tm), lambda i: (i, 0, 0), memory_space=pltpu.SMEM),
            pl.BlockSpec(memory_space=pl.ANY),
            pl.BlockSpec((tm, D_MODEL), lambda i: (i, 0)),
            pl.BlockSpec((tm, ROUTER_LANES), lambda i: (i, 0)),
            pl.BlockSpec((1, D_MODEL), lambda i: (0, 0)),
        ],
        out_specs=pl.BlockSpec((tm, D_MODEL), lambda i: (i, 0)),
        out_shape=jax.ShapeDtypeStruct((te, D_MODEL), F32),
        scratch_shapes=[pltpu.VMEM((2, tm, D_MODEL // 2), U32), pltpu.SemaphoreType.DMA],
        compiler_params=_cparams(("arbitrary",)),
        name="moe_combine",
    )(dest3, ys, h, rw, nw)


def _rope_tables(batch, seq):
    pos = jnp.arange(seq, dtype=F32)
    inv_freq = ROPE_THETA ** (-jnp.arange(0, HEAD_DIM, 2, dtype=F32) / HEAD_DIM)
    ang = pos[:, None] * inv_freq[None, :]
    c, s = jnp.cos(ang), jnp.sin(ang)
    cos = jnp.concatenate([c, c, c, c], axis=1)
    sin = jnp.concatenate([-s, s, -s, s], axis=1)
    return jnp.tile(cos, (batch, 1)), jnp.tile(sin, (batch, 1))


def _token_mixer(x2d, batch, seq, p, cnt_in):
    cos, sin = _rope_tables(batch, seq)
    f, qz, k, vt = _inproj(x2d, p["nmix"], p["w1"], cos, sin, tm=512)
    four = _fourier(f, batch, seq)
    attn = _attention(qz, k, vt, p["lam"], p["sub"], batch, seq, tq=512, ck=2048,
                      kv_block=min(seq, 4096))
    return _merge(x2d, four, attn, p["nmix"], p["wgate"], p["bg"], p["wf"], p["wa"], p["wo"],
                  p["nffn"], p["wr"], cnt_in, tm=256)


def kernel(x_prompt, x_sample, norm_mix_w, w_in, b_gate, w_fourier, lambda_q1, lambda_k1, lambda_q2,
           lambda_k2, subln_w, w_attn, w_out, norm_ffn_w, w_group_router, w_expert_router,
           w_expert_gate, w_expert_up, w_expert_down, norm_final_w):
    wr = jnp.concatenate([w_group_router[0], w_expert_router[0]], axis=1)
    wr = jnp.pad(wr, ((0, 0), (0, ROUTER_LANES - wr.shape[1])))
    p = {
        "nmix": norm_mix_w[0][None, :],
        "w1": w_in[0][:, :2048].astype(BF16),
        "wgate": w_in[0][:, 2048:].astype(BF16),
        "bg": b_gate[0][None, :],
        "wf": w_fourier[0].astype(BF16),
        "wa": w_attn[0].astype(BF16),
        "wo": w_out[0].astype(BF16),
        "nffn": norm_ffn_w[0][None, :],
        "wr": wr,
        "lam": jnp.stack([lambda_q1[0], lambda_k1[0], lambda_q2[0], lambda_k2[0]]),
        "sub": subln_w[0][:, None],
    }
    inputs = [x_prompt, x_sample]
    mixed = []
    cnt = jnp.zeros((1, ROUTER_LANES), F32)
    for x in inputs:
        batch, seq, _ = x.shape
        h, hnp, ri, rw, cnt = _token_mixer(x.reshape(batch * seq, D_MODEL), batch, seq, p, cnt)
        mixed.append((h, hnp, ri, rw))
    outs = _moe_and_final(mixed, cnt, w_expert_gate[0], w_expert_up[0], w_expert_down[0], norm_final_w)
    return tuple(o.reshape(x.shape) for o, x in zip(outs, inputs))


def _moe_and_final(mixed, cnt, w_gate, w_up, w_down, norm_final_w):
    n_assign = 2 * sum(m[0].shape[0] for m in mixed)
    n_blocks = -(-(n_assign + N_EXPERTS * (MOE_ROWS - 1)) // MOE_ROWS)
    n_rows = n_blocks * MOE_ROWS
    counts = cnt[0, EXPERT_LANE0:EXPERT_LANE0 + N_EXPERTS].astype(I32)
    padded = (counts + MOE_ROWS - 1) // MOE_ROWS * MOE_ROWS
    pend = jnp.cumsum(padded)
    pstart = pend - padded
    total = pend[-1]
    bstart = jnp.arange(n_blocks, dtype=I32) * MOE_ROWS
    be = jnp.minimum(jnp.sum((pend[None, :] <= bstart[:, None]).astype(I32), axis=1), N_EXPERTS - 1)
    n_used = (total // MOE_ROWS).astype(I32)
    be = jnp.where(bstart < total, be, be[jnp.maximum(n_used - 1, 0)])
    zstart = pstart + counts
    zend = jnp.where(jnp.arange(N_EXPERTS) == N_EXPERTS - 1, n_rows, pend)
    zlen = (zend - zstart).astype(I32)

    dests = [pstart[m[2][:, 0:2]] + m[2][:, 2:4] for m in mixed]
    xs = _dispatch(zstart, zlen, dests, [m[1] for m in mixed], n_rows, tm=1024)
    ys = _experts(be, n_used[None], xs, w_gate, w_up, w_down)
    return [_combine(dest, ys, m[0], m[3], norm_final_w[None, :], tm=256)
            for m, dest in zip(mixed, dests)]
```

```python
import functools
import math

import numpy as np
import jax
import jax.numpy as jnp
from jax import lax
from jax.experimental import pallas as pl
from jax.experimental.pallas import tpu as pltpu

F32 = jnp.float32
BF16 = jnp.bfloat16
U32 = jnp.uint32
I32 = jnp.int32

D_MODEL = 1024
FOURIER_WIDTH = 512
FOURIER_GROUPS = 4
GROUP_DIM = 128
HEADS = 4
HEAD_DIM = 64
QK_WIDTH = 512
V_DIM = 128
V_WIDTH = 512
ROPE_THETA = 10000.0
N_GROUPS = 4
EXPERTS_PER_GROUP = 8
N_EXPERTS = 32
EXPERT_FF = 1024
NORM_EPS = 1e-6
LAMBDA_INIT = 0.8 - 0.6 * math.exp(0.0)

LANES = 128
ROUTER_LANES = 128
EXPERT_LANE0 = N_GROUPS
MOE_ROWS = 256
DFT_N = 128
VMEM_LIMIT = 56 * 1024 * 1024

LOG2E = 1.4426950408889634
NEG_BIG = -3.0e38


def _cparams(sem, vmem=VMEM_LIMIT, flags=None):
    return pltpu.CompilerParams(dimension_semantics=sem, vmem_limit_bytes=vmem, flags=flags)


def _swap_halves(t):
    n = t.shape[1]
    lane = lax.broadcasted_iota(I32, t.shape, 1)
    first = (lane % HEAD_DIM) < (HEAD_DIM // 2)
    return jnp.where(first, pltpu.roll(t, n - HEAD_DIM // 2, 1), pltpu.roll(t, HEAD_DIM // 2, 1))


def _inproj_kernel(x_ref, nw_ref, w_ref, cos_ref, sin_ref, f_ref, qz_ref, k_ref, vt_ref):
    x = x_ref[...]
    xn = x * lax.rsqrt(jnp.mean(x * x, axis=-1, keepdims=True) + NORM_EPS) * nw_ref[...]
    proj = jnp.dot(xn.astype(BF16), w_ref[...], preferred_element_type=F32)
    f_ref[...] = proj[:, :FOURIER_WIDTH].astype(BF16)
    q = proj[:, FOURIER_WIDTH:FOURIER_WIDTH + QK_WIDTH]
    k = proj[:, FOURIER_WIDTH + QK_WIDTH:FOURIER_WIDTH + 2 * QK_WIDTH]
    v = proj[:, FOURIER_WIDTH + 2 * QK_WIDTH:]
    cos = jnp.concatenate([cos_ref[...]] * (QK_WIDTH // LANES), axis=1)
    sin = jnp.concatenate([sin_ref[...]] * (QK_WIDTH // LANES), axis=1)
    q = q * cos + _swap_halves(q) * sin
    k = k * cos + _swap_halves(k) * sin
    q = q * (HEAD_DIM ** -0.5 * LOG2E)
    lane = lax.broadcasted_iota(I32, (q.shape[0], LANES), 1)
    lo = lane < HEAD_DIM
    vt = v.T.astype(BF16)
    for h in range(HEADS):
        qh = q[:, h * LANES:(h + 1) * LANES]
        qz_ref[2 * h] = jnp.where(lo, qh, 0.0).astype(BF16)
        qz_ref[2 * h + 1] = jnp.where(lo, 0.0, qh).astype(BF16)
        k_ref[h] = k[:, h * LANES:(h + 1) * LANES].astype(BF16)
        vt_ref[0, h] = vt[h * V_DIM:(h + 1) * V_DIM, :]


def _inproj(x, nw, w1, cos, sin, tm):
    te = x.shape[0]
    nt = te // tm
    pos_tiles = cos.shape[0] // tm
    return pl.pallas_call(
        _inproj_kernel,
        grid=(nt,),
        in_specs=[
            pl.BlockSpec((tm, D_MODEL), lambda i: (i, 0)),
            pl.BlockSpec((1, D_MODEL), lambda i: (0, 0)),
            pl.BlockSpec((D_MODEL, 2048), lambda i: (0, 0)),
            pl.BlockSpec((tm, LANES), lambda i: (i % pos_tiles, 0)),
            pl.BlockSpec((tm, LANES), lambda i: (i % pos_tiles, 0)),
        ],
        out_specs=[
            pl.BlockSpec((tm, FOURIER_WIDTH), lambda i: (i, 0)),
            pl.BlockSpec((2 * HEADS, tm, LANES), lambda i: (0, i, 0)),
            pl.BlockSpec((HEADS, tm, LANES), lambda i: (0, i, 0)),
            pl.BlockSpec((1, HEADS, V_DIM, tm), lambda i: (i, 0, 0, 0)),
        ],
        out_shape=[
            jax.ShapeDtypeStruct((te, FOURIER_WIDTH), BF16),
            jax.ShapeDtypeStruct((2 * HEADS, te, LANES), BF16),
            jax.ShapeDtypeStruct((HEADS, te, LANES), BF16),
            jax.ShapeDtypeStruct((nt, HEADS, V_DIM, tm), BF16),
        ],
        compiler_params=_cparams(("parallel",)),
        name="inproj",
    )(x, nw, w1, cos, sin)


@functools.lru_cache(maxsize=None)
def _dft_tables(batch, seq):
    n1 = seq // DFT_N
    assert batch * n1 == DFT_N
    idx = np.arange(n1)
    ang = 2.0 * np.pi * ((idx[:, None] * idx[None, :]) % n1) / n1
    eye = np.eye(batch)
    m1 = np.concatenate([np.kron(eye, np.cos(ang)), -np.kron(eye, np.sin(ang))], axis=0)
    k1 = np.arange(n1)[:, None, None]
    k2 = np.arange(DFT_N)[None, :, None]
    n2 = np.arange(DFT_N)[None, None, :]
    ang2 = 2.0 * np.pi * ((n2 * (k1 + n1 * k2)) % seq) / seq
    er, ei = np.cos(ang2), -np.sin(ang2)
    m2 = np.concatenate([np.concatenate([er, -ei], axis=2), np.concatenate([ei, er], axis=2)], axis=1)
    c = np.arange(GROUP_DIM)
    ang3 = 2.0 * np.pi * ((c[:, None] * c[None, :]) % GROUP_DIM) / GROUP_DIM
    norm = 1.0 / math.sqrt(seq * GROUP_DIM)
    cs = np.concatenate([np.cos(ang3), np.sin(ang3)], axis=0) * norm
    return (jnp.asarray(m1, BF16), jnp.asarray(m2, BF16), jnp.asarray(cs, BF16))


def _dft1_kernel(m_ref, x_ref, y_ref):
    y_ref[...] = jnp.dot(m_ref[...], x_ref[...], preferred_element_type=F32).astype(BF16)


def _dft1(m1, x2d, tn=4096):
    n = x2d.shape[1]
    return pl.pallas_call(
        _dft1_kernel,
        grid=(n // tn,),
        in_specs=[pl.BlockSpec((2 * DFT_N, DFT_N), lambda i: (0, 0)),
                  pl.BlockSpec((DFT_N, tn), lambda i: (0, i))],
        out_specs=pl.BlockSpec((2 * DFT_N, tn), lambda i: (0, i)),
        out_shape=jax.ShapeDtypeStruct((2 * DFT_N, n), BF16),
        compiler_params=_cparams(("parallel",)),
        name="dft_stage1",
    )(m1, x2d)


def _dft2_kernel(m_ref, yr_ref, yi_ref, cs_ref, o_ref, *, rows_per_step):
    for j in range(rows_per_step):
        m = m_ref[j]
        z = (jnp.dot(m[:, :DFT_N], yr_ref[0, j], preferred_element_type=F32)
             + jnp.dot(m[:, DFT_N:], yi_ref[0, j], preferred_element_type=F32))
        zb = z.astype(BF16)
        for g in range(FOURIER_GROUPS):
            zg = jnp.concatenate([zb[:DFT_N, g * LANES:(g + 1) * LANES],
                                  zb[DFT_N:, g * LANES:(g + 1) * LANES]], axis=1)
            o_ref[0, :, (j * FOURIER_GROUPS + g) * LANES:(j * FOURIER_GROUPS + g + 1) * LANES] = (
                jnp.dot(zg, cs_ref[...], preferred_element_type=F32).astype(BF16))


def _dft2(m2, y, cs, batch, seq, rows_per_step=4):
    n1 = seq // DFT_N
    r = rows_per_step
    steps = DFT_N // r
    per_b = n1 // r
    return pl.pallas_call(
        functools.partial(_dft2_kernel, rows_per_step=r),
        grid=(steps,),
        in_specs=[
            pl.BlockSpec((r, 2 * DFT_N, 2 * DFT_N), lambda i: (i % per_b, 0, 0)),
            pl.BlockSpec((1, r, DFT_N, FOURIER_WIDTH), lambda i: (0, i, 0, 0)),
            pl.BlockSpec((1, r, DFT_N, FOURIER_WIDTH), lambda i: (1, i, 0, 0)),
            pl.BlockSpec((2 * GROUP_DIM, GROUP_DIM), lambda i: (0, 0)),
        ],
        out_specs=pl.BlockSpec((1, DFT_N, r * FOURIER_WIDTH), lambda i: (i // per_b, 0, i % per_b)),
        out_shape=jax.ShapeDtypeStruct((batch, DFT_N, n1 * FOURIER_WIDTH), BF16),
        compiler_params=_cparams(("parallel",)),
        name="dft_stage2",
    )(m2, y, y, cs)


def _fourier(f, batch, seq):
    m1, m2, cs = _dft_tables(batch, seq)
    y = _dft1(m1, f.reshape(DFT_N, DFT_N * FOURIER_WIDTH))
    out = _dft2(m2, y.reshape(2, DFT_N, DFT_N, FOURIER_WIDTH), cs, batch, seq)
    return out.reshape(batch * seq, FOURIER_WIDTH)


def _attn_kernel(qz_ref, k_ref, vt_ref, lam_ref, sub_ref, o_ref,
                 s_a, s_b, mx_a, mx_b, m_a, m_b, l_a, l_b, acc_a, acc_b, *, ck, sb, n_chunks):
    j = pl.program_id(2)
    n_items = n_chunks * HEADS

    @pl.when(j == 0)
    def _():
        for m_sc, l_sc, acc_sc in ((m_a, l_a, acc_a), (m_b, l_b, acc_b)):
            m_sc[...] = jnp.full(m_sc.shape, NEG_BIG, F32)
            l_sc[...] = jnp.zeros(l_sc.shape, F32)
            acc_sc[...] = jnp.zeros(acc_sc.shape, F32)

    def split(t):
        t = jnp.asarray(t, I32)
        return lax.shift_right_logical(t, 2), t & (HEADS - 1)

    ckv = vt_ref.shape[-1]
    n_sub = ck // sb
    bufs = ((s_a, mx_a, m_a, l_a, acc_a), (s_b, mx_b, m_b, l_b, acc_b))

    def step(t, do_update, do_scores):
        if do_update:
            c, h = split(t)
        if do_scores:
            c1, h1 = split(t + 1)
        state = []
        for cc, (s_ref, mx_ref, m_sc, l_sc, acc_sc) in enumerate(bufs):
            if do_update:
                m_old = m_sc[h]
                m_new = jnp.maximum(m_old, mx_ref[...])
                state.append([m_old, m_new, None, None, None])
            else:
                state.append([None, None, None, None, None])
        for u in range(n_sub):
            rows = pl.ds(u * sb, sb)
            for cc, (s_ref, mx_ref, m_sc, l_sc, acc_sc) in enumerate(bufs):
                st = state[cc]
                if do_update:
                    p = jnp.exp2(s_ref[rows, :] - st[1])
                    ps = jnp.sum(p, axis=0, keepdims=True)
                    lane0 = (u * sb) % ckv
                    vth = vt_ref[c * (ck // ckv) + (u * sb) // ckv, h, :, lane0:lane0 + sb]
                    d = jnp.dot(vth, p.astype(BF16), preferred_element_type=F32)
                    st[2] = ps if st[2] is None else st[2] + ps
                    st[3] = d if st[3] is None else st[3] + d
                if do_scores:
                    row0 = pl.multiple_of(c1 * ck + u * sb, sb)
                    s = lax.dot_general(k_ref[h1, pl.ds(row0, sb), :], qz_ref[2 * h1 + cc],
                                        (((1,), (1,)), ((), ())), preferred_element_type=F32)
                    s_ref[rows, :] = s
                    mx = jnp.max(s, axis=0, keepdims=True)
                    st[4] = mx if st[4] is None else jnp.maximum(st[4], mx)
        for cc, (s_ref, mx_ref, m_sc, l_sc, acc_sc) in enumerate(bufs):
            m_old, m_new, l_add, pv, mx = state[cc]
            if do_update:
                alpha = jnp.exp2(m_old - m_new)
                l_sc[h] = alpha * l_sc[h] + l_add
                acc_sc[h] = alpha * acc_sc[h] + pv
                m_sc[h] = m_new
            if do_scores:
                mx_ref[...] = mx

    step(-1, False, True)

    def body(t, carry):
        step(t, True, True)
        return carry

    lax.fori_loop(0, n_items - 1, body, 0)
    step(n_items - 1, True, False)

    @pl.when(j == pl.num_programs(2) - 1)
    def _():
        lv = lam_ref[...]
        lam = (jnp.exp(jnp.sum(lv[0:1] * lv[1:2], axis=1, keepdims=True))
               - jnp.exp(jnp.sum(lv[2:3] * lv[3:4], axis=1, keepdims=True)) + LAMBDA_INIT)
        for h in range(HEADS):
            o = acc_a[h] / l_a[h] - lam * (acc_b[h] / l_b[h])
            o = o * lax.rsqrt(jnp.mean(o * o, axis=0, keepdims=True) + NORM_EPS)
            o = o * sub_ref[...] * (1.0 - LAMBDA_INIT)
            o_ref[:, h * V_DIM:(h + 1) * V_DIM] = o.T.astype(BF16)


def _attention(qz, k, vt, lam_vecs, sub_col, batch, seq, tq, ck, sb, kv_block):
    nq = seq // tq
    nkb = seq // kv_block
    n_chunks = kv_block // ck
    ckv = vt.shape[-1]
    stat = pltpu.VMEM((HEADS, 1, tq), F32)
    kv_mode = dict(pipeline_mode=pl.Buffered(1)) if batch * nkb == 1 else {}
    return pl.pallas_call(
        functools.partial(_attn_kernel, ck=ck, sb=sb, n_chunks=n_chunks),
        grid=(batch, nq, nkb),
        in_specs=[
            pl.BlockSpec((2 * HEADS, tq, LANES), lambda b, i, j: (0, b * nq + i, 0)),
            pl.BlockSpec((HEADS, kv_block, LANES), lambda b, i, j: (0, b * nkb + j, 0), **kv_mode),
            pl.BlockSpec((kv_block // ckv, HEADS, V_DIM, ckv), lambda b, i, j: (b * nkb + j, 0, 0, 0),
                         **kv_mode),
            pl.BlockSpec((4, HEAD_DIM), lambda b, i, j: (0, 0)),
            pl.BlockSpec((V_DIM, 1), lambda b, i, j: (0, 0)),
        ],
        out_specs=pl.BlockSpec((tq, V_WIDTH), lambda b, i, j: (b * nq + i, 0)),
        out_shape=jax.ShapeDtypeStruct((batch * seq, V_WIDTH), BF16),
        scratch_shapes=[
            pltpu.VMEM((ck, tq), F32), pltpu.VMEM((ck, tq), F32),
            pltpu.VMEM((1, tq), F32), pltpu.VMEM((1, tq), F32),
            stat, stat, stat, stat,
            pltpu.VMEM((HEADS, V_DIM, tq), F32), pltpu.VMEM((HEADS, V_DIM, tq), F32),
        ],
        compiler_params=_cparams(("parallel", "parallel", "arbitrary")),
        name="diff_attention",
    )(qz, k, vt, lam_vecs, sub_col)


def _pack_bf16_pairs(a, b):
    ab = lax.bitcast_convert_type(a.astype(BF16).astype(F32), U32)
    bb = lax.bitcast_convert_type(b.astype(BF16).astype(F32), U32)
    return (bb & jnp.uint32(0xFFFF0000)) | (ab >> 16)


def _unpack_bf16_pairs(w):
    lo = lax.bitcast_convert_type(w << 16, F32)
    hi = lax.bitcast_convert_type(w & jnp.uint32(0xFFFF0000), F32)
    return lo, hi


def _merge_kernel(x_ref, four_ref, attn_ref, nmix_ref, wg_ref, bg_ref, wf_ref, wa_ref, wo_ref,
                  nffn_ref, wr_ref, cnt_in_ref,
                  h_ref, hnp_ref, ri_ref, rw_ref, cnt_ref, carry_sc):
    i = pl.program_id(0)

    @pl.when(i == 0)
    def _():
        carry_sc[...] = cnt_in_ref[...]

    x = x_ref[...]
    xn = x * lax.rsqrt(jnp.mean(x * x, axis=-1, keepdims=True) + NORM_EPS) * nmix_ref[...]
    gates = jax.nn.sigmoid(jnp.dot(xn.astype(BF16), wg_ref[...], preferred_element_type=F32)
                           + bg_ref[...])
    bf = jnp.dot(four_ref[...], wf_ref[...], preferred_element_type=F32)
    ba = jnp.dot(attn_ref[...], wa_ref[...], preferred_element_type=F32)
    merged = gates[:, :D_MODEL] * bf + gates[:, D_MODEL:] * ba
    h = x + jnp.dot(merged.astype(BF16), wo_ref[...], preferred_element_type=F32)
    h_ref[...] = h
    hn = h * lax.rsqrt(jnp.mean(h * h, axis=-1, keepdims=True) + NORM_EPS) * nffn_ref[...]
    hnp_ref[...] = _pack_bf16_pairs(hn[:, :D_MODEL // 2], hn[:, D_MODEL // 2:])

    tm = hn.shape[0]
    hn_hi = hn.astype(BF16)
    hn_lo = (hn - hn_hi.astype(F32)).astype(BF16)
    parts = jnp.dot(jnp.concatenate([hn_hi, hn_lo], axis=0), wr_ref[...],
                    preferred_element_type=F32)
    lg = (((parts[tm:, ROUTER_LANES:] + parts[:tm, ROUTER_LANES:]) + parts[tm:, :ROUTER_LANES])
          + parts[:tm, :ROUTER_LANES])
    lane = lax.broadcasted_iota(I32, (tm, ROUTER_LANES), 1)
    gmask = lane < N_GROUPS
    gl = jnp.where(gmask, lg, NEG_BIG)
    gmax = jnp.max(gl, axis=1, keepdims=True)
    g_idx = jnp.min(jnp.where(gl == gmax, lane, ROUTER_LANES), axis=1, keepdims=True)
    g_w = 1.0 / jnp.sum(jnp.where(gmask, jnp.exp(gl - gmax), 0.0), axis=1, keepdims=True)
    e_lane = lane - EXPERT_LANE0
    emask = (e_lane >= 0) & (e_lane < N_EXPERTS) & ((e_lane // EXPERTS_PER_GROUP) == g_idx)
    el = jnp.where(emask, lg, NEG_BIG)
    v1 = jnp.max(el, axis=1, keepdims=True)
    i1 = jnp.min(jnp.where(el == v1, lane, ROUTER_LANES), axis=1, keepdims=True)
    el2 = jnp.where(lane == i1, NEG_BIG, el)
    v2 = jnp.max(el2, axis=1, keepdims=True)
    i2 = jnp.min(jnp.where(el2 == v2, lane, ROUTER_LANES), axis=1, keepdims=True)
    s21 = jnp.exp(v2 - v1)
    w1 = g_w / (1.0 + s21)
    w2 = g_w * s21 / (1.0 + s21)

    oh1 = lane == i1
    oh2 = lane == i2
    ohs = (oh1 | oh2).astype(BF16)
    row = lax.broadcasted_iota(I32, (tm, tm), 0)
    col = lax.broadcasted_iota(I32, (tm, tm), 1)
    tril = (col < row).astype(BF16)
    before = jnp.dot(tril, ohs, preferred_element_type=F32) + carry_sc[...]
    r1 = jnp.sum(jnp.where(oh1, before, 0.0), axis=1, keepdims=True)
    r2 = jnp.sum(jnp.where(oh2, before, 0.0), axis=1, keepdims=True)
    carry_sc[...] = carry_sc[...] + jnp.sum(ohs.astype(F32), axis=0, keepdims=True)
    cnt_ref[...] = carry_sc[...]

    e1 = i1 - EXPERT_LANE0
    e2 = i2 - EXPERT_LANE0
    ri = jnp.where(lane == 0, e1.astype(F32), jnp.where(lane == 1, e2.astype(F32),
         jnp.where(lane == 2, r1, jnp.where(lane == 3, r2, 0.0))))
    ri_ref[...] = ri.T
    rw_ref[...] = jnp.where(lane == 0, w1, jnp.where(lane == 1, w2, 0.0))


def _merge(x, four, attn, nmix, wg, bg, wf, wa, wo, nffn, wr, cnt_in, tm):
    te = x.shape[0]
    nt = te // tm
    const = lambda i: (0, 0)
    tile = lambda i: (i, 0)
    return pl.pallas_call(
        _merge_kernel,
        grid=(nt,),
        in_specs=[
            pl.BlockSpec((tm, D_MODEL), tile),
            pl.BlockSpec((tm, FOURIER_WIDTH), tile),
            pl.BlockSpec((tm, V_WIDTH), tile),
            pl.BlockSpec((1, D_MODEL), const),
            pl.BlockSpec((D_MODEL, 2 * D_MODEL), const),
            pl.BlockSpec((1, 2 * D_MODEL), const),
            pl.BlockSpec((FOURIER_WIDTH, D_MODEL), const),
            pl.BlockSpec((V_WIDTH, D_MODEL), const),
            pl.BlockSpec((D_MODEL, D_MODEL), const),
            pl.BlockSpec((1, D_MODEL), const),
            pl.BlockSpec((D_MODEL, 2 * ROUTER_LANES), const),
            pl.BlockSpec((1, ROUTER_LANES), const),
        ],
        out_specs=[
            pl.BlockSpec((tm, D_MODEL), tile),
            pl.BlockSpec((tm, D_MODEL // 2), tile),
            pl.BlockSpec((ROUTER_LANES, tm), lambda i: (0, i)),
            pl.BlockSpec((tm, ROUTER_LANES), tile),
            pl.BlockSpec((1, ROUTER_LANES), const),
        ],
        out_shape=[
            jax.ShapeDtypeStruct((te, D_MODEL), F32),
            jax.ShapeDtypeStruct((te, D_MODEL // 2), U32),
            jax.ShapeDtypeStruct((ROUTER_LANES, te), F32),
            jax.ShapeDtypeStruct((te, ROUTER_LANES), F32),
            jax.ShapeDtypeStruct((1, ROUTER_LANES), F32),
        ],
        scratch_shapes=[pltpu.VMEM((1, ROUTER_LANES), F32)],
        compiler_params=_cparams(("arbitrary",)),
        name="merge_router",
    )(x, four, attn, nmix, wg, bg, wf, wa, wo, nffn, wr, cnt_in)


def _tile_dest(dest, tm):
    nt = dest.shape[1] // tm
    return dest.reshape(2, nt, tm).transpose(1, 0, 2).reshape(nt, 1, 2 * tm)


def _dispatch_kernel(zs_ref, zl_ref, dest_ref, *rest, tm, tiles):
    n_src = len(tiles)
    src_refs = rest[:n_src]
    xs_ref, zero_sc, sem, zsem = rest[n_src:]
    i = pl.program_id(0)

    def copy_tile(src_ref):
        def row_copy(t, slot):
            return pltpu.make_async_copy(src_ref.at[pl.ds(t, 1)],
                                         xs_ref.at[pl.ds(dest_ref[0, 0, slot * tm + t], 1)], sem)

        def start(t, c):
            row_copy(t, 0).start()
            row_copy(t, 1).start()
            return c

        lax.fori_loop(0, tm, start, 0, unroll=8)

        def wait(t, c):
            row_copy(t, 0).wait()
            row_copy(t, 1).wait()
            return c

        lax.fori_loop(0, tm, wait, 0, unroll=8)

    first_tile = 0
    for src_ref, n_tiles in zip(src_refs, tiles):
        @pl.when((i >= first_tile) & (i < first_tile + n_tiles))
        def _(src_ref=src_ref):
            copy_tile(src_ref)
        first_tile += n_tiles

    @pl.when(i == 0)
    def _():
        zero_sc[...] = jnp.zeros(zero_sc.shape, U32)

        def zero_copy(r):
            return pltpu.make_async_copy(zero_sc.at[pl.ds(0, 1)], xs_ref.at[pl.ds(r, 1)], zsem)

        def per_expert(e, c):
            def zstart(r, c2):
                zero_copy(zs_ref[e] + r).start()
                return c2
            lax.fori_loop(0, zl_ref[e], zstart, 0)

            def zwait(r, c2):
                zero_copy(zs_ref[e] + r).wait()
                return c2
            lax.fori_loop(0, zl_ref[e], zwait, 0)
            return c
        lax.fori_loop(0, N_EXPERTS, per_expert, 0)


def _dispatch(zstart, zlen, dests, srcs, n_rows, tm):
    tiles = tuple(s.shape[0] // tm for s in srcs)
    dest3 = jnp.concatenate([_tile_dest(d, tm) for d in dests], axis=0)
    src_specs = []
    first_tile = 0
    for n in tiles:
        src_specs.append(pl.BlockSpec(
            (tm, D_MODEL // 2), lambda i, zs, zl, ft=first_tile, n=n: (jnp.clip(i - ft, 0, n - 1), 0)))
        first_tile += n
    grid_spec = pltpu.PrefetchScalarGridSpec(
        num_scalar_prefetch=2,
        grid=(sum(tiles),),
        in_specs=[pl.BlockSpec((1, 1, 2 * tm), lambda i, zs, zl: (i, 0, 0), memory_space=pltpu.SMEM)]
        + src_specs,
        out_specs=pl.BlockSpec(memory_space=pl.ANY),
        scratch_shapes=[pltpu.VMEM((8, D_MODEL // 2), U32), pltpu.SemaphoreType.DMA,
                        pltpu.SemaphoreType.DMA],
    )
    return pl.pallas_call(
        functools.partial(_dispatch_kernel, tm=tm, tiles=tiles),
        grid_spec=grid_spec,
        out_shape=jax.ShapeDtypeStruct((n_rows, D_MODEL // 2), U32),
        compiler_params=_cparams(("arbitrary",)),
        name="moe_dispatch",
    )(zstart, zlen, dest3, *srcs)


def _expert_kernel(be_ref, nu_ref, ord_ref, nxt_ref, xs_ref, wg_hbm, wu_hbm, wd_hbm, o_ref,
                   wbuf, wgb, wub, wdb, sem):
    b = pl.program_id(0)
    used = b < nu_ref[0]
    e = be_ref[b]
    new_expert = (b == 0) | (e != be_ref[jnp.maximum(b - 1, 0)])
    slot = ord_ref[b] & 1

    def fetch(expert, s):
        return [pltpu.make_async_copy(w.at[expert], wbuf.at[s, i], sem.at[s])
                for i, w in enumerate((wg_hbm, wu_hbm, wd_hbm))]

    @pl.when((b == 0) & used)
    def _():
        for cp in fetch(e, 0):
            cp.start()

    @pl.when(used & new_expert)
    def _():
        for cp in fetch(e, slot):
            cp.wait()

        @pl.when(nxt_ref[b] >= 0)
        def _():
            for cp in fetch(nxt_ref[b], 1 - slot):
                cp.start()

        rows = 128

        def cast(r, c):
            r0 = pl.multiple_of(r * rows, rows)
            wgb[pl.ds(r0, rows), :] = wbuf[slot, 0, pl.ds(r0, rows), :].astype(BF16)
            wub[pl.ds(r0, rows), :] = wbuf[slot, 1, pl.ds(r0, rows), :].astype(BF16)
            wdb[pl.ds(r0, rows), :] = wbuf[slot, 2, pl.ds(r0, rows), :].astype(BF16)
            return c
        lax.fori_loop(0, D_MODEL // rows, cast, 0)

    @pl.when(used)
    def _():
        lo, hi = _unpack_bf16_pairs(xs_ref[...])
        x = jnp.concatenate([lo, hi], axis=1).astype(BF16)
        g = jnp.dot(x, wgb[...], preferred_element_type=F32)
        u = jnp.dot(x, wub[...], preferred_element_type=F32)
        hmid = (g * jax.nn.sigmoid(g) * u).astype(BF16)
        y = jnp.dot(hmid, wdb[...], preferred_element_type=F32)
        o_ref[...] = _pack_bf16_pairs(y[:, :D_MODEL // 2], y[:, D_MODEL // 2:])

    @pl.when(jnp.logical_not(used))
    def _():
        o_ref[...] = jnp.zeros(o_ref.shape, U32)


def _experts(block_expert, n_used, block_ord, block_next, xs, wg, wu, wd):
    assert EXPERT_FF == D_MODEL
    n_blocks = xs.shape[0] // MOE_ROWS
    row_spec = pl.BlockSpec((MOE_ROWS, D_MODEL // 2), lambda b, *_: (b, 0))
    grid_spec = pltpu.PrefetchScalarGridSpec(
        num_scalar_prefetch=4,
        grid=(n_blocks,),
        in_specs=[row_spec] + [pl.BlockSpec(memory_space=pl.ANY)] * 3,
        out_specs=row_spec,
        scratch_shapes=[pltpu.VMEM((2, 3, D_MODEL, EXPERT_FF), F32),
                        pltpu.VMEM((D_MODEL, EXPERT_FF), BF16), pltpu.VMEM((D_MODEL, EXPERT_FF), BF16),
                        pltpu.VMEM((EXPERT_FF, D_MODEL), BF16), pltpu.SemaphoreType.DMA((2,))],
    )
    return pl.pallas_call(
        _expert_kernel,
        grid_spec=grid_spec,
        out_shape=jax.ShapeDtypeStruct((xs.shape[0], D_MODEL // 2), U32),
        compiler_params=_cparams(("arbitrary",)),
        name="moe_experts",
    )(block_expert, n_used, block_ord, block_next, xs, wg, wu, wd)


def _combine_kernel(dest_ref, ys_ref, h_ref, rw_ref, nw_ref, o_ref, buf, sem, *, tm):
    def row_copy(t, slot):
        return pltpu.make_async_copy(ys_ref.at[pl.ds(dest_ref[0, 0, slot * tm + t], 1)],
                                     buf.at[slot, pl.ds(t, 1)], sem)

    def start(t, c):
        row_copy(t, 0).start()
        row_copy(t, 1).start()
        return c

    lax.fori_loop(0, tm, start, 0, unroll=8)

    def wait(t, c):
        row_copy(t, 0).wait()
        row_copy(t, 1).wait()
        return c

    lax.fori_loop(0, tm, wait, 0, unroll=8)

    rw = rw_ref[...]
    w1 = rw[:, 0:1]
    w2 = rw[:, 1:2]
    lo1, hi1 = _unpack_bf16_pairs(buf[0])
    lo2, hi2 = _unpack_bf16_pairs(buf[1])
    y = jnp.concatenate([lo1 * w1 + lo2 * w2, hi1 * w1 + hi2 * w2], axis=1)
    h = h_ref[...] + y
    o_ref[...] = h * lax.rsqrt(jnp.mean(h * h, axis=-1, keepdims=True) + NORM_EPS) * nw_ref[...]


def _combine(dest, ys, h, rw, nw, tm):
    te = h.shape[0]
    nt = te // tm
    dest3 = _tile_dest(dest, tm)
    return pl.pallas_call(
        functools.partial(_combine_kernel, tm=tm),
        grid=(nt,),
        in_specs=[
            pl.BlockSpec((1, 1, 2 * tm), lambda i: (i, 0, 0), memory_space=pltpu.SMEM),
            pl.BlockSpec(memory_space=pl.ANY),
            pl.BlockSpec((tm, D_MODEL), lambda i: (i, 0)),
            pl.BlockSpec((tm, ROUTER_LANES), lambda i: (i, 0)),
            pl.BlockSpec((1, D_MODEL), lambda i: (0, 0)),
        ],
        out_specs=pl.BlockSpec((tm, D_MODEL), lambda i: (i, 0)),
        out_shape=jax.ShapeDtypeStruct((te, D_MODEL), F32),
        scratch_shapes=[pltpu.VMEM((2, tm, D_MODEL // 2), U32), pltpu.SemaphoreType.DMA],
        compiler_params=_cparams(("arbitrary",)),
        name="moe_combine",
    )(dest3, ys, h, rw, nw)


def _split_router_weight(wr):
    wr = jnp.pad(wr, ((0, 0), (0, ROUTER_LANES - wr.shape[1])))
    hi = wr.astype(BF16)
    lo = (wr - hi.astype(F32)).astype(BF16)
    return jnp.concatenate([hi, lo], axis=1)


def _rope_tables(seq):
    pos = jnp.arange(seq, dtype=F32)
    inv_freq = ROPE_THETA ** (-jnp.arange(0, HEAD_DIM, 2, dtype=F32) / HEAD_DIM)
    ang = pos[:, None] * inv_freq[None, :]
    c, s = jnp.cos(ang), jnp.sin(ang)
    cos = jnp.concatenate([c, c, c, c], axis=1)
    sin = jnp.concatenate([-s, s, -s, s], axis=1)
    return cos, sin


def _token_mixer(x2d, batch, seq, p, cnt_in):
    cos, sin = _rope_tables(seq)
    f, qz, k, vt = _inproj(x2d, p["nmix"], p["w1"], cos, sin, tm=512)
    four = _fourier(f, batch, seq)
    attn = _attention(qz, k, vt, p["lam"], p["sub"], batch, seq, tq=512, ck=min(2048, seq // 2),
                      sb=256, kv_block=seq)
    return _merge(x2d, four, attn, p["nmix"], p["wgate"], p["bg"], p["wf"], p["wa"], p["wo"],
                  p["nffn"], p["wr"], cnt_in, tm=512)


def kernel(x_prompt, x_sample, norm_mix_w, w_in, b_gate, w_fourier, lambda_q1, lambda_k1, lambda_q2,
           lambda_k2, subln_w, w_attn, w_out, norm_ffn_w, w_group_router, w_expert_router,
           w_expert_gate, w_expert_up, w_expert_down, norm_final_w):
    wr = jnp.concatenate([w_group_router[0], w_expert_router[0]], axis=1)
    wr = _split_router_weight(wr)
    p = {
        "nmix": norm_mix_w[0][None, :],
        "w1": w_in[0][:, :2048].astype(BF16),
        "wgate": w_in[0][:, 2048:].astype(BF16),
        "bg": b_gate[0][None, :],
        "wf": w_fourier[0].astype(BF16),
        "wa": w_attn[0].astype(BF16),
        "wo": w_out[0].astype(BF16),
        "nffn": norm_ffn_w[0][None, :],
        "wr": wr,
        "lam": jnp.stack([lambda_q1[0], lambda_k1[0], lambda_q2[0], lambda_k2[0]]),
        "sub": subln_w[0][:, None],
    }
    inputs = [x_prompt, x_sample]
    mixed = []
    cnt = jnp.zeros((1, ROUTER_LANES), F32)
    for x in inputs:
        batch, seq, _ = x.shape
        h, hnp, ri, rw, cnt = _token_mixer(x.reshape(batch * seq, D_MODEL), batch, seq, p, cnt)
        mixed.append((h, hnp, ri, rw))
    outs = _moe_and_final(mixed, cnt, w_expert_gate[0], w_expert_up[0], w_expert_down[0], norm_final_w)
    return tuple(o.reshape(x.shape) for o, x in zip(outs, inputs))


def _moe_and_final(mixed, cnt, w_gate, w_up, w_down, norm_final_w):
    n_assign = 2 * sum(m[0].shape[0] for m in mixed)
    n_blocks = -(-(n_assign + N_EXPERTS * (MOE_ROWS - 1)) // MOE_ROWS)
    n_rows = n_blocks * MOE_ROWS
    counts = cnt[0, EXPERT_LANE0:EXPERT_LANE0 + N_EXPERTS].astype(I32)
    padded = (counts + MOE_ROWS - 1) // MOE_ROWS * MOE_ROWS
    pend = jnp.cumsum(padded)
    pstart = pend - padded
    total = pend[-1]
    bstart = jnp.arange(n_blocks, dtype=I32) * MOE_ROWS
    be = jnp.minimum(jnp.sum((pend[None, :] <= bstart[:, None]).astype(I32), axis=1), N_EXPERTS - 1)
    n_used = (total // MOE_ROWS).astype(I32)
    be = jnp.where(bstart < total, be, be[jnp.maximum(n_used - 1, 0)])
    block_ord = jnp.cumsum(jnp.concatenate([jnp.zeros((1,), I32), (be[1:] != be[:-1]).astype(I32)]))
    eidx = jnp.arange(N_EXPERTS, dtype=I32)
    later_used = (eidx[None, :] > eidx[:, None]) & (padded[None, :] > 0)
    next_expert = jnp.min(jnp.where(later_used, eidx[None, :], N_EXPERTS), axis=1)
    next_expert = jnp.where(next_expert == N_EXPERTS, -1, next_expert)
    block_next = next_expert[be]
    zstart = pstart + counts
    zend = jnp.where(jnp.arange(N_EXPERTS) == N_EXPERTS - 1, n_rows, pend)
    zlen = (zend - zstart).astype(I32)

    dests = [pstart[m[2][0:2].astype(I32)] + m[2][2:4].astype(I32) for m in mixed]
    xs = _dispatch(zstart, zlen, dests, [m[1] for m in mixed], n_rows, tm=1024)
    ys = _experts(be, n_used[None], block_ord, block_next, xs, w_gate, w_up, w_down)
    return [_combine(dest, ys, m[0], m[3], norm_final_w[None, :], tm=512)
            for m, dest in zip(mixed, dests)]
```

```python
import functools
import math

import numpy as np
import jax
import jax.numpy as jnp
from jax import lax
from jax.experimental import pallas as pl
from jax.experimental.pallas import tpu as pltpu

F32 = jnp.float32
BF16 = jnp.bfloat16
U32 = jnp.uint32
I32 = jnp.int32

D_MODEL = 1024
FOURIER_WIDTH = 512
FOURIER_GROUPS = 4
GROUP_DIM = 128
HEADS = 4
HEAD_DIM = 64
QK_WIDTH = 512
V_DIM = 128
V_WIDTH = 512
ROPE_THETA = 10000.0
N_GROUPS = 4
EXPERTS_PER_GROUP = 8
N_EXPERTS = 32
EXPERT_FF = 1024
NORM_EPS = 1e-6
LAMBDA_INIT = 0.8 - 0.6 * math.exp(0.0)

LANES = 128
ROUTER_LANES = 128
EXPERT_LANE0 = N_GROUPS
MOE_ROWS = 256
DFT_N = 128
VMEM_LIMIT = 56 * 1024 * 1024

LOG2E = 1.4426950408889634
NEG_BIG = -3.0e38


def _cparams(sem, vmem=VMEM_LIMIT, flags=None):
    return pltpu.CompilerParams(dimension_semantics=sem, vmem_limit_bytes=vmem, flags=flags)


def _swap_halves(t):
    n = t.shape[1]
    lane = lax.broadcasted_iota(I32, t.shape, 1)
    first = (lane % HEAD_DIM) < (HEAD_DIM // 2)
    return jnp.where(first, pltpu.roll(t, n - HEAD_DIM // 2, 1), pltpu.roll(t, HEAD_DIM // 2, 1))


def _inproj_kernel(x_ref, nw_ref, w_ref, cos_ref, sin_ref, f_ref, qz_ref, k_ref, vt_ref):
    x = x_ref[...]
    xn = x * lax.rsqrt(jnp.mean(x * x, axis=-1, keepdims=True) + NORM_EPS) * nw_ref[...]
    proj = jnp.dot(xn.astype(BF16), w_ref[...], preferred_element_type=F32)
    f_ref[...] = proj[:, :FOURIER_WIDTH].astype(BF16)
    q = proj[:, FOURIER_WIDTH:FOURIER_WIDTH + QK_WIDTH]
    k = proj[:, FOURIER_WIDTH + QK_WIDTH:FOURIER_WIDTH + 2 * QK_WIDTH]
    v = proj[:, FOURIER_WIDTH + 2 * QK_WIDTH:]
    cos = jnp.concatenate([cos_ref[...]] * (QK_WIDTH // LANES), axis=1)
    sin = jnp.concatenate([sin_ref[...]] * (QK_WIDTH // LANES), axis=1)
    q = q * cos + _swap_halves(q) * sin
    k = k * cos + _swap_halves(k) * sin
    q = q * (HEAD_DIM ** -0.5 * LOG2E)
    lane = lax.broadcasted_iota(I32, (q.shape[0], LANES), 1)
    lo = lane < HEAD_DIM
    vt = v.T.astype(BF16)
    for h in range(HEADS):
        qh = q[:, h * LANES:(h + 1) * LANES]
        qz_ref[2 * h] = jnp.where(lo, qh, 0.0).astype(BF16)
        qz_ref[2 * h + 1] = jnp.where(lo, 0.0, qh).astype(BF16)
        k_ref[h] = k[:, h * LANES:(h + 1) * LANES].astype(BF16)
        vt_ref[0, h] = vt[h * V_DIM:(h + 1) * V_DIM, :]


def _inproj(x, nw, w1, cos, sin, tm):
    te = x.shape[0]
    nt = te // tm
    pos_tiles = cos.shape[0] // tm
    return pl.pallas_call(
        _inproj_kernel,
        grid=(nt,),
        in_specs=[
            pl.BlockSpec((tm, D_MODEL), lambda i: (i, 0)),
            pl.BlockSpec((1, D_MODEL), lambda i: (0, 0)),
            pl.BlockSpec((D_MODEL, 2048), lambda i: (0, 0)),
            pl.BlockSpec((tm, LANES), lambda i: (i % pos_tiles, 0)),
            pl.BlockSpec((tm, LANES), lambda i: (i % pos_tiles, 0)),
        ],
        out_specs=[
            pl.BlockSpec((tm, FOURIER_WIDTH), lambda i: (i, 0)),
            pl.BlockSpec((2 * HEADS, tm, LANES), lambda i: (0, i, 0)),
            pl.BlockSpec((HEADS, tm, LANES), lambda i: (0, i, 0)),
            pl.BlockSpec((1, HEADS, V_DIM, tm), lambda i: (i, 0, 0, 0)),
        ],
        out_shape=[
            jax.ShapeDtypeStruct((te, FOURIER_WIDTH), BF16),
            jax.ShapeDtypeStruct((2 * HEADS, te, LANES), BF16),
            jax.ShapeDtypeStruct((HEADS, te, LANES), BF16),
            jax.ShapeDtypeStruct((nt, HEADS, V_DIM, tm), BF16),
        ],
        compiler_params=_cparams(("parallel",)),
        name="inproj",
    )(x, nw, w1, cos, sin)


@functools.lru_cache(maxsize=None)
def _dft_tables(batch, seq):
    n1 = seq // DFT_N
    assert batch * n1 == DFT_N
    idx = np.arange(n1)
    ang = 2.0 * np.pi * ((idx[:, None] * idx[None, :]) % n1) / n1
    eye = np.eye(batch)
    m1 = np.concatenate([np.kron(eye, np.cos(ang)), -np.kron(eye, np.sin(ang))], axis=0)
    k1 = np.arange(n1)[:, None, None]
    k2 = np.arange(DFT_N)[None, :, None]
    n2 = np.arange(DFT_N)[None, None, :]
    ang2 = 2.0 * np.pi * ((n2 * (k1 + n1 * k2)) % seq) / seq
    er, ei = np.cos(ang2), -np.sin(ang2)
    m2 = np.concatenate([np.concatenate([er, -ei], axis=2), np.concatenate([ei, er], axis=2)], axis=1)
    c = np.arange(GROUP_DIM)
    ang3 = 2.0 * np.pi * ((c[:, None] * c[None, :]) % GROUP_DIM) / GROUP_DIM
    norm = 1.0 / math.sqrt(seq * GROUP_DIM)
    cs = np.concatenate([np.cos(ang3), np.sin(ang3)], axis=0) * norm
    return (jnp.asarray(m1, BF16), jnp.asarray(m2, BF16), jnp.asarray(cs, BF16))


def _dft1_kernel(m_ref, x_ref, y_ref):
    y_ref[...] = jnp.dot(m_ref[...], x_ref[...], preferred_element_type=F32).astype(BF16)


def _dft1(m1, x2d, tn=4096):
    n = x2d.shape[1]
    return pl.pallas_call(
        _dft1_kernel,
        grid=(n // tn,),
        in_specs=[pl.BlockSpec((2 * DFT_N, DFT_N), lambda i: (0, 0)),
                  pl.BlockSpec((DFT_N, tn), lambda i: (0, i))],
        out_specs=pl.BlockSpec((2 * DFT_N, tn), lambda i: (0, i)),
        out_shape=jax.ShapeDtypeStruct((2 * DFT_N, n), BF16),
        compiler_params=_cparams(("parallel",)),
        name="dft_stage1",
    )(m1, x2d)


def _dft2_kernel(m_ref, yr_ref, yi_ref, cs_ref, o_ref, *, rows_per_step):
    for j in range(rows_per_step):
        m = m_ref[j]
        z = (jnp.dot(m[:, :DFT_N], yr_ref[0, j], preferred_element_type=F32)
             + jnp.dot(m[:, DFT_N:], yi_ref[0, j], preferred_element_type=F32))
        zb = z.astype(BF16)
        for g in range(FOURIER_GROUPS):
            zg = jnp.concatenate([zb[:DFT_N, g * LANES:(g + 1) * LANES],
                                  zb[DFT_N:, g * LANES:(g + 1) * LANES]], axis=1)
            o_ref[0, :, (j * FOURIER_GROUPS + g) * LANES:(j * FOURIER_GROUPS + g + 1) * LANES] = (
                jnp.dot(zg, cs_ref[...], preferred_element_type=F32).astype(BF16))


def _dft2(m2, y, cs, batch, seq, rows_per_step=4):
    n1 = seq // DFT_N
    r = rows_per_step
    steps = DFT_N // r
    per_b = n1 // r
    return pl.pallas_call(
        functools.partial(_dft2_kernel, rows_per_step=r),
        grid=(steps,),
        in_specs=[
            pl.BlockSpec((r, 2 * DFT_N, 2 * DFT_N), lambda i: (i % per_b, 0, 0)),
            pl.BlockSpec((1, r, DFT_N, FOURIER_WIDTH), lambda i: (0, i, 0, 0)),
            pl.BlockSpec((1, r, DFT_N, FOURIER_WIDTH), lambda i: (1, i, 0, 0)),
            pl.BlockSpec((2 * GROUP_DIM, GROUP_DIM), lambda i: (0, 0)),
        ],
        out_specs=pl.BlockSpec((1, DFT_N, r * FOURIER_WIDTH), lambda i: (i // per_b, 0, i % per_b)),
        out_shape=jax.ShapeDtypeStruct((batch, DFT_N, n1 * FOURIER_WIDTH), BF16),
        compiler_params=_cparams(("parallel",)),
        name="dft_stage2",
    )(m2, y, y, cs)


def _fourier(f, batch, seq):
    m1, m2, cs = _dft_tables(batch, seq)
    y = _dft1(m1, f.reshape(DFT_N, DFT_N * FOURIER_WIDTH))
    out = _dft2(m2, y.reshape(2, DFT_N, DFT_N, FOURIER_WIDTH), cs, batch, seq)
    return out.reshape(batch * seq, FOURIER_WIDTH)


def _attn_kernel(qz_ref, k_ref, vt_ref, lam_ref, sub_ref, o_ref,
                 s_a, s_b, mx_a, mx_b, m_a, m_b, l_a, l_b, acc_a, acc_b, *, ck, sb, n_chunks):
    j = pl.program_id(2)
    n_items = n_chunks * HEADS

    @pl.when(j == 0)
    def _():
        for m_sc, l_sc, acc_sc in ((m_a, l_a, acc_a), (m_b, l_b, acc_b)):
            m_sc[...] = jnp.full(m_sc.shape, NEG_BIG, F32)
            l_sc[...] = jnp.zeros(l_sc.shape, F32)
            acc_sc[...] = jnp.zeros(acc_sc.shape, F32)

    def split(t):
        t = jnp.asarray(t, I32)
        return lax.shift_right_logical(t, 2), t & (HEADS - 1)

    ckv = vt_ref.shape[-1]
    n_sub = ck // sb
    bufs = ((s_a, mx_a, m_a, l_a, acc_a), (s_b, mx_b, m_b, l_b, acc_b))

    def step(t, do_update, do_scores):
        if do_update:
            c, h = split(t)
        if do_scores:
            c1, h1 = split(t + 1)
        state = []
        for cc, (s_ref, mx_ref, m_sc, l_sc, acc_sc) in enumerate(bufs):
            if do_update:
                m_old = m_sc[h]
                m_new = jnp.maximum(m_old, mx_ref[...])
                state.append([m_old, m_new, None, None, None])
            else:
                state.append([None, None, None, None, None])
        for u in range(n_sub):
            rows = pl.ds(u * sb, sb)
            for cc, (s_ref, mx_ref, m_sc, l_sc, acc_sc) in enumerate(bufs):
                st = state[cc]
                if do_update:
                    p = jnp.exp2(s_ref[rows, :] - st[1])
                    ps = jnp.sum(p, axis=0, keepdims=True)
                    lane0 = (u * sb) % ckv
                    vth = vt_ref[c * (ck // ckv) + (u * sb) // ckv, h, :, lane0:lane0 + sb]
                    d = jnp.dot(vth, p.astype(BF16), preferred_element_type=F32)
                    st[2] = ps if st[2] is None else st[2] + ps
                    st[3] = d if st[3] is None else st[3] + d
                if do_scores:
                    row0 = pl.multiple_of(c1 * ck + u * sb, sb)
                    s = lax.dot_general(k_ref[h1, pl.ds(row0, sb), :], qz_ref[2 * h1 + cc],
                                        (((1,), (1,)), ((), ())), preferred_element_type=F32)
                    s_ref[rows, :] = s
                    mx = jnp.max(s, axis=0, keepdims=True)
                    st[4] = mx if st[4] is None else jnp.maximum(st[4], mx)
        for cc, (s_ref, mx_ref, m_sc, l_sc, acc_sc) in enumerate(bufs):
            m_old, m_new, l_add, pv, mx = state[cc]
            if do_update:
                alpha = jnp.exp2(m_old - m_new)
                l_sc[h] = alpha * l_sc[h] + l_add
                acc_sc[h] = alpha * acc_sc[h] + pv
                m_sc[h] = m_new
            if do_scores:
                mx_ref[...] = mx

    step(-1, False, True)

    def body(t, carry):
        step(t, True, True)
        return carry

    lax.fori_loop(0, n_items - 1, body, 0)
    step(n_items - 1, True, False)

    @pl.when(j == pl.num_programs(2) - 1)
    def _():
        lv = lam_ref[...]
        lam = (jnp.exp(jnp.sum(lv[0:1] * lv[1:2], axis=1, keepdims=True))
               - jnp.exp(jnp.sum(lv[2:3] * lv[3:4], axis=1, keepdims=True)) + LAMBDA_INIT)
        for h in range(HEADS):
            o = acc_a[h] / l_a[h] - lam * (acc_b[h] / l_b[h])
            o = o * lax.rsqrt(jnp.mean(o * o, axis=0, keepdims=True) + NORM_EPS)
            o = o * sub_ref[...] * (1.0 - LAMBDA_INIT)
            o_ref[:, h * V_DIM:(h + 1) * V_DIM] = o.T.astype(BF16)


def _attention(qz, k, vt, lam_vecs, sub_col, batch, seq, tq, ck, sb, kv_block):
    nq = seq // tq
    nkb = seq // kv_block
    n_chunks = kv_block // ck
    ckv = vt.shape[-1]
    stat = pltpu.VMEM((HEADS, 1, tq), F32)
    kv_mode = dict(pipeline_mode=pl.Buffered(1)) if batch * nkb == 1 else {}
    return pl.pallas_call(
        functools.partial(_attn_kernel, ck=ck, sb=sb, n_chunks=n_chunks),
        grid=(batch, nq, nkb),
        in_specs=[
            pl.BlockSpec((2 * HEADS, tq, LANES), lambda b, i, j: (0, b * nq + i, 0)),
            pl.BlockSpec((HEADS, kv_block, LANES), lambda b, i, j: (0, b * nkb + j, 0), **kv_mode),
            pl.BlockSpec((kv_block // ckv, HEADS, V_DIM, ckv), lambda b, i, j: (b * nkb + j, 0, 0, 0),
                         **kv_mode),
            pl.BlockSpec((4, HEAD_DIM), lambda b, i, j: (0, 0)),
            pl.BlockSpec((V_DIM, 1), lambda b, i, j: (0, 0)),
        ],
        out_specs=pl.BlockSpec((tq, V_WIDTH), lambda b, i, j: (b * nq + i, 0)),
        out_shape=jax.ShapeDtypeStruct((batch * seq, V_WIDTH), BF16),
        scratch_shapes=[
            pltpu.VMEM((ck, tq), F32), pltpu.VMEM((ck, tq), F32),
            pltpu.VMEM((1, tq), F32), pltpu.VMEM((1, tq), F32),
            stat, stat, stat, stat,
            pltpu.VMEM((HEADS, V_DIM, tq), F32), pltpu.VMEM((HEADS, V_DIM, tq), F32),
        ],
        compiler_params=_cparams(("parallel", "parallel", "arbitrary")),
        name="diff_attention",
    )(qz, k, vt, lam_vecs, sub_col)


def _pack_bf16_pairs(a, b):
    ab = lax.bitcast_convert_type(a.astype(BF16).astype(F32), U32)
    bb = lax.bitcast_convert_type(b.astype(BF16).astype(F32), U32)
    return (bb & jnp.uint32(0xFFFF0000)) | (ab >> 16)


def _unpack_bf16_pairs(w):
    lo = lax.bitcast_convert_type(w << 16, F32)
    hi = lax.bitcast_convert_type(w & jnp.uint32(0xFFFF0000), F32)
    return lo, hi


def _merge_kernel(x_ref, four_ref, attn_ref, nmix_ref, wg_ref, bg_ref, wf_ref, wa_ref, wo_ref,
                  nffn_ref, wr_ref, cnt_in_ref,
                  h_ref, hnp_ref, ri_ref, rw_ref, cnt_ref, carry_sc):
    i = pl.program_id(0)

    @pl.when(i == 0)
    def _():
        carry_sc[...] = cnt_in_ref[...]

    x = x_ref[...]
    xn = x * lax.rsqrt(jnp.mean(x * x, axis=-1, keepdims=True) + NORM_EPS) * nmix_ref[...]
    gates = jax.nn.sigmoid(jnp.dot(xn.astype(BF16), wg_ref[...], preferred_element_type=F32)
                           + bg_ref[...])
    bf = jnp.dot(four_ref[...], wf_ref[...], preferred_element_type=F32)
    ba = jnp.dot(attn_ref[...], wa_ref[...], preferred_element_type=F32)
    merged = gates[:, :D_MODEL] * bf + gates[:, D_MODEL:] * ba
    h = x + jnp.dot(merged.astype(BF16), wo_ref[...], preferred_element_type=F32)
    h_ref[...] = h
    hn = h * lax.rsqrt(jnp.mean(h * h, axis=-1, keepdims=True) + NORM_EPS) * nffn_ref[...]
    hnp_ref[...] = _pack_bf16_pairs(hn[:, :D_MODEL // 2], hn[:, D_MODEL // 2:])

    tm = hn.shape[0]
    hn_hi = hn.astype(BF16)
    hn_lo = (hn - hn_hi.astype(F32)).astype(BF16)
    parts = jnp.dot(jnp.concatenate([hn_hi, hn_lo], axis=0), wr_ref[...],
                    preferred_element_type=F32)
    lg = (((parts[tm:, ROUTER_LANES:] + parts[:tm, ROUTER_LANES:]) + parts[tm:, :ROUTER_LANES])
          + parts[:tm, :ROUTER_LANES])
    lane = lax.broadcasted_iota(I32, (tm, ROUTER_LANES), 1)
    gmask = lane < N_GROUPS
    gl = jnp.where(gmask, lg, NEG_BIG)
    gmax = jnp.max(gl, axis=1, keepdims=True)
    g_idx = jnp.min(jnp.where(gl == gmax, lane, ROUTER_LANES), axis=1, keepdims=True)
    g_w = 1.0 / jnp.sum(jnp.where(gmask, jnp.exp(gl - gmax), 0.0), axis=1, keepdims=True)
    e_lane = lane - EXPERT_LANE0
    emask = (e_lane >= 0) & (e_lane < N_EXPERTS) & ((e_lane // EXPERTS_PER_GROUP) == g_idx)
    el = jnp.where(emask, lg, NEG_BIG)
    v1 = jnp.max(el, axis=1, keepdims=True)
    i1 = jnp.min(jnp.where(el == v1, lane, ROUTER_LANES), axis=1, keepdims=True)
    el2 = jnp.where(lane == i1, NEG_BIG, el)
    v2 = jnp.max(el2, axis=1, keepdims=True)
    i2 = jnp.min(jnp.where(el2 == v2, lane, ROUTER_LANES), axis=1, keepdims=True)
    s21 = jnp.exp(v2 - v1)
    w1 = g_w / (1.0 + s21)
    w2 = g_w * s21 / (1.0 + s21)

    oh1 = lane == i1
    oh2 = lane == i2
    ohs = (oh1 | oh2).astype(BF16)
    row = lax.broadcasted_iota(I32, (tm, tm), 0)
    col = lax.broadcasted_iota(I32, (tm, tm), 1)
    tril = (col < row).astype(BF16)
    before = jnp.dot(tril, ohs, preferred_element_type=F32) + carry_sc[...]
    r1 = jnp.sum(jnp.where(oh1, before, 0.0), axis=1, keepdims=True)
    r2 = jnp.sum(jnp.where(oh2, before, 0.0), axis=1, keepdims=True)
    carry_sc[...] = carry_sc[...] + jnp.sum(ohs.astype(F32), axis=0, keepdims=True)
    cnt_ref[...] = carry_sc[...]

    e1 = i1 - EXPERT_LANE0
    e2 = i2 - EXPERT_LANE0
    ri = jnp.where(lane == 0, e1.astype(F32), jnp.where(lane == 1, e2.astype(F32),
         jnp.where(lane == 2, r1, jnp.where(lane == 3, r2, 0.0))))
    ri_ref[...] = ri.T
    rw_ref[...] = jnp.where(lane == 0, w1, jnp.where(lane == 1, w2, 0.0))


def _merge(x, four, attn, nmix, wg, bg, wf, wa, wo, nffn, wr, cnt_in, tm):
    te = x.shape[0]
    nt = te // tm
    const = lambda i: (0, 0)
    tile = lambda i: (i, 0)
    return pl.pallas_call(
        _merge_kernel,
        grid=(nt,),
        in_specs=[
            pl.BlockSpec((tm, D_MODEL), tile),
            pl.BlockSpec((tm, FOURIER_WIDTH), tile),
            pl.BlockSpec((tm, V_WIDTH), tile),
            pl.BlockSpec((1, D_MODEL), const),
            pl.BlockSpec((D_MODEL, 2 * D_MODEL), const),
            pl.BlockSpec((1, 2 * D_MODEL), const),
            pl.BlockSpec((FOURIER_WIDTH, D_MODEL), const),
            pl.BlockSpec((V_WIDTH, D_MODEL), const),
            pl.BlockSpec((D_MODEL, D_MODEL), const),
            pl.BlockSpec((1, D_MODEL), const),
            pl.BlockSpec((D_MODEL, 2 * ROUTER_LANES), const),
            pl.BlockSpec((1, ROUTER_LANES), const),
        ],
        out_specs=[
            pl.BlockSpec((tm, D_MODEL), tile),
            pl.BlockSpec((tm, D_MODEL // 2), tile),
            pl.BlockSpec((ROUTER_LANES, tm), lambda i: (0, i)),
            pl.BlockSpec((tm, ROUTER_LANES), tile),
            pl.BlockSpec((1, ROUTER_LANES), const),
        ],
        out_shape=[
            jax.ShapeDtypeStruct((te, D_MODEL), F32),
            jax.ShapeDtypeStruct((te, D_MODEL // 2), U32),
            jax.ShapeDtypeStruct((ROUTER_LANES, te), F32),
            jax.ShapeDtypeStruct((te, ROUTER_LANES), F32),
            jax.ShapeDtypeStruct((1, ROUTER_LANES), F32),
        ],
        scratch_shapes=[pltpu.VMEM((1, ROUTER_LANES), F32)],
        compiler_params=_cparams(("arbitrary",)),
        name="merge_router",
    )(x, four, attn, nmix, wg, bg, wf, wa, wo, nffn, wr, cnt_in)


def _tile_dest(dest, tm):
    nt = dest.shape[1] // tm
    return dest.reshape(2, nt, tm).transpose(1, 0, 2).reshape(nt, 1, 2 * tm)


def _dispatch_kernel(zs_ref, zl_ref, dest_ref, *rest, tm, tiles):
    n_src = len(tiles)
    src_refs = rest[:n_src]
    xs_ref, zero_sc, sem, zsem = rest[n_src:]
    i = pl.program_id(0)

    def copy_tile(src_ref):
        def row_copy(t, slot):
            return pltpu.make_async_copy(src_ref.at[pl.ds(t, 1)],
                                         xs_ref.at[pl.ds(dest_ref[0, 0, slot * tm + t], 1)], sem)

        def start(t, c):
            row_copy(t, 0).start()
            row_copy(t, 1).start()
            return c

        lax.fori_loop(0, tm, start, 0, unroll=8)

        def wait(t, c):
            row_copy(t, 0).wait()
            row_copy(t, 1).wait()
            return c

        lax.fori_loop(0, tm, wait, 0, unroll=8)

    first_tile = 0
    for src_ref, n_tiles in zip(src_refs, tiles):
        @pl.when((i >= first_tile) & (i < first_tile + n_tiles))
        def _(src_ref=src_ref):
            copy_tile(src_ref)
        first_tile += n_tiles

    @pl.when(i == 0)
    def _():
        zero_sc[...] = jnp.zeros(zero_sc.shape, U32)

        def zero_copy(r):
            return pltpu.make_async_copy(zero_sc.at[pl.ds(0, 1)], xs_ref.at[pl.ds(r, 1)], zsem)

        def per_expert(e, c):
            def zstart(r, c2):
                zero_copy(zs_ref[e] + r).start()
                return c2
            lax.fori_loop(0, zl_ref[e], zstart, 0)

            def zwait(r, c2):
                zero_copy(zs_ref[e] + r).wait()
                return c2
            lax.fori_loop(0, zl_ref[e], zwait, 0)
            return c
        lax.fori_loop(0, N_EXPERTS, per_expert, 0)


def _dispatch(zstart, zlen, dests, srcs, n_rows, tm):
    tiles = tuple(s.shape[0] // tm for s in srcs)
    dest3 = jnp.concatenate([_tile_dest(d, tm) for d in dests], axis=0)
    src_specs = []
    first_tile = 0
    for n in tiles:
        src_specs.append(pl.BlockSpec(
            (tm, D_MODEL // 2), lambda i, zs, zl, ft=first_tile, n=n: (jnp.clip(i - ft, 0, n - 1), 0)))
        first_tile += n
    grid_spec = pltpu.PrefetchScalarGridSpec(
        num_scalar_prefetch=2,
        grid=(sum(tiles),),
        in_specs=[pl.BlockSpec((1, 1, 2 * tm), lambda i, zs, zl: (i, 0, 0), memory_space=pltpu.SMEM)]
        + src_specs,
        out_specs=pl.BlockSpec(memory_space=pl.ANY),
        scratch_shapes=[pltpu.VMEM((8, D_MODEL // 2), U32), pltpu.SemaphoreType.DMA,
                        pltpu.SemaphoreType.DMA],
    )
    return pl.pallas_call(
        functools.partial(_dispatch_kernel, tm=tm, tiles=tiles),
        grid_spec=grid_spec,
        out_shape=jax.ShapeDtypeStruct((n_rows, D_MODEL // 2), U32),
        compiler_params=_cparams(("arbitrary",)),
        name="moe_dispatch",
    )(zstart, zlen, dest3, *srcs)


def _expert_kernel(be_ref, nu_ref, ord_ref, nxt_ref, xs_ref, wg_hbm, wu_hbm, wd_hbm, o_ref,
                   wbuf, wgb, wub, wdb, sem):
    b = pl.program_id(0)
    used = b < nu_ref[0]
    e = be_ref[b]
    new_expert = (b == 0) | (e != be_ref[jnp.maximum(b - 1, 0)])
    slot = ord_ref[b] & 1

    def fetch(expert, s):
        return [pltpu.make_async_copy(w.at[expert], wbuf.at[s, i], sem.at[s])
                for i, w in enumerate((wg_hbm, wu_hbm, wd_hbm))]

    @pl.when((b == 0) & used)
    def _():
        for cp in fetch(e, 0):
            cp.start()

    @pl.when(used & new_expert)
    def _():
        for cp in fetch(e, slot):
            cp.wait()

        @pl.when(nxt_ref[b] >= 0)
        def _():
            for cp in fetch(nxt_ref[b], 1 - slot):
                cp.start()

        rows = 128

        def cast(r, c):
            r0 = pl.multiple_of(r * rows, rows)
            wgb[pl.ds(r0, rows), :] = wbuf[slot, 0, pl.ds(r0, rows), :].astype(BF16)
            wub[pl.ds(r0, rows), :] = wbuf[slot, 1, pl.ds(r0, rows), :].astype(BF16)
            wdb[pl.ds(r0, rows), :] = wbuf[slot, 2, pl.ds(r0, rows), :].astype(BF16)
            return c
        lax.fori_loop(0, D_MODEL // rows, cast, 0)

    @pl.when(used)
    def _():
        lo, hi = _unpack_bf16_pairs(xs_ref[...])
        x = jnp.concatenate([lo, hi], axis=1).astype(BF16)
        g = jnp.dot(x, wgb[...], preferred_element_type=F32)
        u = jnp.dot(x, wub[...], preferred_element_type=F32)
        hmid = (g * jax.nn.sigmoid(g) * u).astype(BF16)
        y = jnp.dot(hmid, wdb[...], preferred_element_type=F32)
        o_ref[...] = _pack_bf16_pairs(y[:, :D_MODEL // 2], y[:, D_MODEL // 2:])

    @pl.when(jnp.logical_not(used))
    def _():
        o_ref[...] = jnp.zeros(o_ref.shape, U32)


def _experts(block_expert, n_used, block_ord, block_next, xs, wg, wu, wd):
    assert EXPERT_FF == D_MODEL
    n_blocks = xs.shape[0] // MOE_ROWS
    row_spec = pl.BlockSpec((MOE_ROWS, D_MODEL // 2), lambda b, *_: (b, 0))
    grid_spec = pltpu.PrefetchScalarGridSpec(
        num_scalar_prefetch=4,
        grid=(n_blocks,),
        in_specs=[row_spec] + [pl.BlockSpec(memory_space=pl.ANY)] * 3,
        out_specs=row_spec,
        scratch_shapes=[pltpu.VMEM((2, 3, D_MODEL, EXPERT_FF), F32),
                        pltpu.VMEM((D_MODEL, EXPERT_FF), BF16), pltpu.VMEM((D_MODEL, EXPERT_FF), BF16),
                        pltpu.VMEM((EXPERT_FF, D_MODEL), BF16), pltpu.SemaphoreType.DMA((2,))],
    )
    return pl.pallas_call(
        _expert_kernel,
        grid_spec=grid_spec,
        out_shape=jax.ShapeDtypeStruct((xs.shape[0], D_MODEL // 2), U32),
        compiler_params=_cparams(("arbitrary",)),
        name="moe_experts",
    )(block_expert, n_used, block_ord, block_next, xs, wg, wu, wd)


def _combine_kernel(dest_ref, ys_ref, h_ref, rw_ref, nw_ref, o_ref, buf, sem, *, tm):
    def row_copy(t, slot):
        return pltpu.make_async_copy(ys_ref.at[pl.ds(dest_ref[0, 0, slot * tm + t], 1)],
                                     buf.at[slot, pl.ds(t, 1)], sem)

    def start(t, c):
        row_copy(t, 0).start()
        row_copy(t, 1).start()
        return c

    lax.fori_loop(0, tm, start, 0, unroll=8)

    def wait(t, c):
        row_copy(t, 0).wait()
        row_copy(t, 1).wait()
        return c

    lax.fori_loop(0, tm, wait, 0, unroll=8)

    rw = rw_ref[...]
    w1 = rw[:, 0:1]
    w2 = rw[:, 1:2]
    lo1, hi1 = _unpack_bf16_pairs(buf[0])
    lo2, hi2 = _unpack_bf16_pairs(buf[1])
    y = jnp.concatenate([lo1 * w1 + lo2 * w2, hi1 * w1 + hi2 * w2], axis=1)
    h = h_ref[...] + y
    o_ref[...] = h * lax.rsqrt(jnp.mean(h * h, axis=-1, keepdims=True) + NORM_EPS) * nw_ref[...]


def _combine(dest, ys, h, rw, nw, tm):
    te = h.shape[0]
    nt = te // tm
    dest3 = _tile_dest(dest, tm)
    return pl.pallas_call(
        functools.partial(_combine_kernel, tm=tm),
        grid=(nt,),
        in_specs=[
            pl.BlockSpec((1, 1, 2 * tm), lambda i: (i, 0, 0), memory_space=pltpu.SMEM),
            pl.BlockSpec(memory_space=pl.ANY),
            pl.BlockSpec((tm, D_MODEL), lambda i: (i, 0)),
            pl.BlockSpec((tm, ROUTER_LANES), lambda i: (i, 0)),
            pl.BlockSpec((1, D_MODEL), lambda i: (0, 0)),
        ],
        out_specs=pl.BlockSpec((tm, D_MODEL), lambda i: (i, 0)),
        out_shape=jax.ShapeDtypeStruct((te, D_MODEL), F32),
        scratch_shapes=[pltpu.VMEM((2, tm, D_MODEL // 2), U32), pltpu.SemaphoreType.DMA],
        compiler_params=_cparams(("arbitrary",)),
        name="moe_combine",
    )(dest3, ys, h, rw, nw)


def _split_router_weight(wr):
    wr = jnp.pad(wr, ((0, 0), (0, ROUTER_LANES - wr.shape[1])))
    hi = wr.astype(BF16)
    lo = (wr - hi.astype(F32)).astype(BF16)
    return jnp.concatenate([hi, lo], axis=1)


def _rope_tables(seq):
    pos = jnp.arange(seq, dtype=F32)
    inv_freq = ROPE_THETA ** (-jnp.arange(0, HEAD_DIM, 2, dtype=F32) / HEAD_DIM)
    ang = pos[:, None] * inv_freq[None, :]
    c, s = jnp.cos(ang), jnp.sin(ang)
    cos = jnp.concatenate([c, c, c, c], axis=1)
    sin = jnp.concatenate([-s, s, -s, s], axis=1)
    return cos, sin


def _token_mixer(x2d, batch, seq, p, cnt_in):
    cos, sin = _rope_tables(seq)
    f, qz, k, vt = _inproj(x2d, p["nmix"], p["w1"], cos, sin, tm=512)
    four = _fourier(f, batch, seq)
    attn = _attention(qz, k, vt, p["lam"], p["sub"], batch, seq, tq=512, ck=min(2048, seq // 2),
                      sb=256, kv_block=seq)
    return _merge(x2d, four, attn, p["nmix"], p["wgate"], p["bg"], p["wf"], p["wa"], p["wo"],
                  p["nffn"], p["wr"], cnt_in, tm=512)


def kernel(x_prompt, x_sample, norm_mix_w, w_in, b_gate, w_fourier, lambda_q1, lambda_k1, lambda_q2,
           lambda_k2, subln_w, w_attn, w_out, norm_ffn_w, w_group_router, w_expert_router,
           w_expert_gate, w_expert_up, w_expert_down, norm_final_w):
    wr = jnp.concatenate([w_group_router[0], w_expert_router[0]], axis=1)
    wr = _split_router_weight(wr)
    p = {
        "nmix": norm_mix_w[0][None, :],
        "w1": w_in[0][:, :2048].astype(BF16),
        "wgate": w_in[0][:, 2048:].astype(BF16),
        "bg": b_gate[0][None, :],
        "wf": w_fourier[0].astype(BF16),
        "wa": w_attn[0].astype(BF16),
        "wo": w_out[0].astype(BF16),
        "nffn": norm_ffn_w[0][None, :],
        "wr": wr,
        "lam": jnp.stack([lambda_q1[0], lambda_k1[0], lambda_q2[0], lambda_k2[0]]),
        "sub": subln_w[0][:, None],
    }
    inputs = [x_prompt, x_sample]
    mixed = []
    cnt = jnp.zeros((1, ROUTER_LANES), F32)
    for x in inputs:
        batch, seq, _ = x.shape
        h, hnp, ri, rw, cnt = _token_mixer(x.reshape(batch * seq, D_MODEL), batch, seq, p, cnt)
        mixed.append((h, hnp, ri, rw))
    outs = _moe_and_final(mixed, cnt, w_expert_gate[0], w_expert_up[0], w_expert_down[0], norm_final_w)
    return tuple(o.reshape(x.shape) for o, x in zip(outs, inputs))


def _moe_and_final(mixed, cnt, w_gate, w_up, w_down, norm_final_w):
    n_assign = 2 * sum(m[0].shape[0] for m in mixed)
    n_blocks = -(-(n_assign + N_EXPERTS * (MOE_ROWS - 1)) // MOE_ROWS)
    n_rows = n_blocks * MOE_ROWS
    counts = cnt[0, EXPERT_LANE0:EXPERT_LANE0 + N_EXPERTS].astype(I32)
    padded = (counts + MOE_ROWS - 1) // MOE_ROWS * MOE_ROWS
    pend = jnp.cumsum(padded)
    pstart = pend - padded
    total = pend[-1]
    bstart = jnp.arange(n_blocks, dtype=I32) * MOE_ROWS
    be = jnp.minimum(jnp.sum((pend[None, :] <= bstart[:, None]).astype(I32), axis=1), N_EXPERTS - 1)
    n_used = (total // MOE_ROWS).astype(I32)
    be = jnp.where(bstart < total, be, be[jnp.maximum(n_used - 1, 0)])
    block_ord = jnp.cumsum(jnp.concatenate([jnp.zeros((1,), I32), (be[1:] != be[:-1]).astype(I32)]))
    eidx = jnp.arange(N_EXPERTS, dtype=I32)
    later_used = (eidx[None, :] > eidx[:, None]) & (padded[None, :] > 0)
    next_expert = jnp.min(jnp.where(later_used, eidx[None, :], N_EXPERTS), axis=1)
    next_expert = jnp.where(next_expert == N_EXPERTS, -1, next_expert)
    block_next = next_expert[be]
    zstart = pstart + counts
    zend = jnp.where(jnp.arange(N_EXPERTS) == N_EXPERTS - 1, n_rows, pend)
    zlen = (zend - zstart).astype(I32)

    def row_of(expert, rank):
        hit = expert[None].astype(I32) == eidx[:, None, None]
        return jnp.sum(jnp.where(hit, pstart[:, None, None], 0), axis=0) + rank.astype(I32)

    dests = [row_of(m[2][0:2], m[2][2:4]) for m in mixed]
    xs = _dispatch(zstart, zlen, dests, [m[1] for m in mixed], n_rows, tm=1024)
    ys = _experts(be, n_used[None], block_ord, block_next, xs, w_gate, w_up, w_down)
    return [_combine(dest, ys, m[0], m[3], norm_final_w[None, :], tm=512)
            for m, dest in zip(mixed, dests)]
```

```python
import functools
import math

import numpy as np
import jax
import jax.numpy as jnp
from jax import lax
from jax.experimental import pallas as pl
from jax.experimental.pallas import tpu as pltpu

F32 = jnp.float32
BF16 = jnp.bfloat16
U32 = jnp.uint32
I32 = jnp.int32

D_MODEL = 1024
FOURIER_WIDTH = 512
FOURIER_GROUPS = 4
GROUP_DIM = 128
HEADS = 4
HEAD_DIM = 64
QK_WIDTH = 512
V_DIM = 128
V_WIDTH = 512
ROPE_THETA = 10000.0
N_GROUPS = 4
EXPERTS_PER_GROUP = 8
N_EXPERTS = 32
EXPERT_FF = 1024
NORM_EPS = 1e-6
LAMBDA_INIT = 0.8 - 0.6 * math.exp(0.0)

LANES = 128
ROUTER_LANES = 128
EXPERT_LANE0 = N_GROUPS
MOE_ROWS = 256
DFT_N = 128
VMEM_LIMIT = 56 * 1024 * 1024

LOG2E = 1.4426950408889634
NEG_BIG = -3.0e38


def _cparams(sem, vmem=VMEM_LIMIT, flags=None):
    return pltpu.CompilerParams(dimension_semantics=sem, vmem_limit_bytes=vmem, flags=flags)


def _swap_halves(t):
    n = t.shape[1]
    lane = lax.broadcasted_iota(I32, t.shape, 1)
    first = (lane % HEAD_DIM) < (HEAD_DIM // 2)
    return jnp.where(first, pltpu.roll(t, n - HEAD_DIM // 2, 1), pltpu.roll(t, HEAD_DIM // 2, 1))


def _inproj_kernel(x_ref, nw_ref, w_ref, cos_ref, sin_ref, f_ref, qz_ref, k_ref, vt_ref):
    x = x_ref[...]
    xn = x * lax.rsqrt(jnp.mean(x * x, axis=-1, keepdims=True) + NORM_EPS) * nw_ref[...]
    proj = jnp.dot(xn.astype(BF16), w_ref[...], preferred_element_type=F32)
    f_ref[...] = proj[:, :FOURIER_WIDTH].astype(BF16)
    q = proj[:, FOURIER_WIDTH:FOURIER_WIDTH + QK_WIDTH]
    k = proj[:, FOURIER_WIDTH + QK_WIDTH:FOURIER_WIDTH + 2 * QK_WIDTH]
    v = proj[:, FOURIER_WIDTH + 2 * QK_WIDTH:]
    cos = jnp.concatenate([cos_ref[...]] * (QK_WIDTH // LANES), axis=1)
    sin = jnp.concatenate([sin_ref[...]] * (QK_WIDTH // LANES), axis=1)
    q = q * cos + _swap_halves(q) * sin
    k = k * cos + _swap_halves(k) * sin
    q = q * (HEAD_DIM ** -0.5 * LOG2E)
    lane = lax.broadcasted_iota(I32, (q.shape[0], LANES), 1)
    lo = lane < HEAD_DIM
    vt = v.T.astype(BF16)
    for h in range(HEADS):
        qh = q[:, h * LANES:(h + 1) * LANES]
        qz_ref[2 * h] = jnp.where(lo, qh, 0.0).astype(BF16)
        qz_ref[2 * h + 1] = jnp.where(lo, 0.0, qh).astype(BF16)
        k_ref[h] = k[:, h * LANES:(h + 1) * LANES].astype(BF16)
        vt_ref[0, h] = vt[h * V_DIM:(h + 1) * V_DIM, :]


def _inproj(x, nw, w1, cos, sin, tm):
    te = x.shape[0]
    nt = te // tm
    pos_tiles = cos.shape[0] // tm
    return pl.pallas_call(
        _inproj_kernel,
        grid=(nt,),
        in_specs=[
            pl.BlockSpec((tm, D_MODEL), lambda i: (i, 0)),
            pl.BlockSpec((1, D_MODEL), lambda i: (0, 0)),
            pl.BlockSpec((D_MODEL, 2048), lambda i: (0, 0)),
            pl.BlockSpec((tm, LANES), lambda i: (i % pos_tiles, 0)),
            pl.BlockSpec((tm, LANES), lambda i: (i % pos_tiles, 0)),
        ],
        out_specs=[
            pl.BlockSpec((tm, FOURIER_WIDTH), lambda i: (i, 0)),
            pl.BlockSpec((2 * HEADS, tm, LANES), lambda i: (0, i, 0)),
            pl.BlockSpec((HEADS, tm, LANES), lambda i: (0, i, 0)),
            pl.BlockSpec((1, HEADS, V_DIM, tm), lambda i: (i, 0, 0, 0)),
        ],
        out_shape=[
            jax.ShapeDtypeStruct((te, FOURIER_WIDTH), BF16),
            jax.ShapeDtypeStruct((2 * HEADS, te, LANES), BF16),
            jax.ShapeDtypeStruct((HEADS, te, LANES), BF16),
            jax.ShapeDtypeStruct((nt, HEADS, V_DIM, tm), BF16),
        ],
        compiler_params=_cparams(("parallel",)),
        name="inproj",
    )(x, nw, w1, cos, sin)


@functools.lru_cache(maxsize=None)
def _dft_tables(batch, seq):
    n1 = seq // DFT_N
    assert batch * n1 == DFT_N
    idx = np.arange(n1)
    ang = 2.0 * np.pi * ((idx[:, None] * idx[None, :]) % n1) / n1
    eye = np.eye(batch)
    m1 = np.concatenate([np.kron(eye, np.cos(ang)), -np.kron(eye, np.sin(ang))], axis=0)
    k1 = np.arange(n1)[:, None, None]
    k2 = np.arange(DFT_N)[None, :, None]
    n2 = np.arange(DFT_N)[None, None, :]
    ang2 = 2.0 * np.pi * ((n2 * (k1 + n1 * k2)) % seq) / seq
    er, ei = np.cos(ang2), -np.sin(ang2)
    m2 = np.concatenate([np.concatenate([er, -ei], axis=2), np.concatenate([ei, er], axis=2)], axis=1)
    c = np.arange(GROUP_DIM)
    ang3 = 2.0 * np.pi * ((c[:, None] * c[None, :]) % GROUP_DIM) / GROUP_DIM
    norm = 1.0 / math.sqrt(seq * GROUP_DIM)
    cs = np.concatenate([np.cos(ang3), np.sin(ang3)], axis=0) * norm
    return (jnp.asarray(m1, BF16), jnp.asarray(m2, BF16), jnp.asarray(cs, BF16))


def _dft1_kernel(m_ref, x_ref, y_ref):
    y_ref[...] = jnp.dot(m_ref[...], x_ref[...], preferred_element_type=F32).astype(BF16)


def _dft1(m1, x2d, tn=4096):
    n = x2d.shape[1]
    return pl.pallas_call(
        _dft1_kernel,
        grid=(n // tn,),
        in_specs=[pl.BlockSpec((2 * DFT_N, DFT_N), lambda i: (0, 0)),
                  pl.BlockSpec((DFT_N, tn), lambda i: (0, i))],
        out_specs=pl.BlockSpec((2 * DFT_N, tn), lambda i: (0, i)),
        out_shape=jax.ShapeDtypeStruct((2 * DFT_N, n), BF16),
        compiler_params=_cparams(("parallel",)),
        name="dft_stage1",
    )(m1, x2d)


def _dft2_kernel(m_ref, yr_ref, yi_ref, cs_ref, o_ref, *, rows_per_step):
    for j in range(rows_per_step):
        m = m_ref[j]
        z = (jnp.dot(m[:, :DFT_N], yr_ref[0, j], preferred_element_type=F32)
             + jnp.dot(m[:, DFT_N:], yi_ref[0, j], preferred_element_type=F32))
        zb = z.astype(BF16)
        for g in range(FOURIER_GROUPS):
            zg = jnp.concatenate([zb[:DFT_N, g * LANES:(g + 1) * LANES],
                                  zb[DFT_N:, g * LANES:(g + 1) * LANES]], axis=1)
            o_ref[0, :, (j * FOURIER_GROUPS + g) * LANES:(j * FOURIER_GROUPS + g + 1) * LANES] = (
                jnp.dot(zg, cs_ref[...], preferred_element_type=F32).astype(BF16))


def _dft2(m2, y, cs, batch, seq, rows_per_step=4):
    n1 = seq // DFT_N
    r = rows_per_step
    steps = DFT_N // r
    per_b = n1 // r
    return pl.pallas_call(
        functools.partial(_dft2_kernel, rows_per_step=r),
        grid=(steps,),
        in_specs=[
            pl.BlockSpec((r, 2 * DFT_N, 2 * DFT_N), lambda i: (i % per_b, 0, 0)),
            pl.BlockSpec((1, r, DFT_N, FOURIER_WIDTH), lambda i: (0, i, 0, 0)),
            pl.BlockSpec((1, r, DFT_N, FOURIER_WIDTH), lambda i: (1, i, 0, 0)),
            pl.BlockSpec((2 * GROUP_DIM, GROUP_DIM), lambda i: (0, 0)),
        ],
        out_specs=pl.BlockSpec((1, DFT_N, r * FOURIER_WIDTH), lambda i: (i // per_b, 0, i % per_b)),
        out_shape=jax.ShapeDtypeStruct((batch, DFT_N, n1 * FOURIER_WIDTH), BF16),
        compiler_params=_cparams(("parallel",)),
        name="dft_stage2",
    )(m2, y, y, cs)


def _fourier(f, batch, seq):
    m1, m2, cs = _dft_tables(batch, seq)
    y = _dft1(m1, f.reshape(DFT_N, DFT_N * FOURIER_WIDTH))
    out = _dft2(m2, y.reshape(2, DFT_N, DFT_N, FOURIER_WIDTH), cs, batch, seq)
    return out.reshape(batch * seq, FOURIER_WIDTH)


def _attn_kernel(qz_ref, k_ref, vt_ref, lam_ref, sub_ref, o_ref,
                 s_a, s_b, mx_a, mx_b, m_a, m_b, l_a, l_b, acc_a, acc_b, *, ck, sb, n_chunks):
    j = pl.program_id(2)
    n_items = n_chunks * HEADS

    @pl.when(j == 0)
    def _():
        for m_sc, l_sc, acc_sc in ((m_a, l_a, acc_a), (m_b, l_b, acc_b)):
            m_sc[...] = jnp.full(m_sc.shape, NEG_BIG, F32)
            l_sc[...] = jnp.zeros(l_sc.shape, F32)
            acc_sc[...] = jnp.zeros(acc_sc.shape, F32)

    def split(t):
        t = jnp.asarray(t, I32)
        return lax.shift_right_logical(t, 2), t & (HEADS - 1)

    ckv = vt_ref.shape[-1]
    n_sub = ck // sb
    bufs = ((s_a, mx_a, m_a, l_a, acc_a), (s_b, mx_b, m_b, l_b, acc_b))

    def step(t, do_update, do_scores):
        if do_update:
            c, h = split(t)
        if do_scores:
            c1, h1 = split(t + 1)
        state = []
        for cc, (s_ref, mx_ref, m_sc, l_sc, acc_sc) in enumerate(bufs):
            if do_update:
                m_old = m_sc[h]
                m_new = jnp.maximum(m_old, mx_ref[...])
                state.append([m_old, m_new, None, None, None])
            else:
                state.append([None, None, None, None, None])
        for u in range(n_sub):
            rows = pl.ds(u * sb, sb)
            for cc, (s_ref, mx_ref, m_sc, l_sc, acc_sc) in enumerate(bufs):
                st = state[cc]
                if do_update:
                    p = jnp.exp2(s_ref[rows, :] - st[1])
                    ps = jnp.sum(p, axis=0, keepdims=True)
                    lane0 = (u * sb) % ckv
                    vth = vt_ref[c * (ck // ckv) + (u * sb) // ckv, h, :, lane0:lane0 + sb]
                    d = jnp.dot(vth, p.astype(BF16), preferred_element_type=F32)
                    st[2] = ps if st[2] is None else st[2] + ps
                    st[3] = d if st[3] is None else st[3] + d
                if do_scores:
                    row0 = pl.multiple_of(c1 * ck + u * sb, sb)
                    s = lax.dot_general(k_ref[h1, pl.ds(row0, sb), :], qz_ref[2 * h1 + cc],
                                        (((1,), (1,)), ((), ())), preferred_element_type=F32)
                    s_ref[rows, :] = s
                    mx = jnp.max(s, axis=0, keepdims=True)
                    st[4] = mx if st[4] is None else jnp.maximum(st[4], mx)
        for cc, (s_ref, mx_ref, m_sc, l_sc, acc_sc) in enumerate(bufs):
            m_old, m_new, l_add, pv, mx = state[cc]
            if do_update:
                alpha = jnp.exp2(m_old - m_new)
                l_sc[h] = alpha * l_sc[h] + l_add
                acc_sc[h] = alpha * acc_sc[h] + pv
                m_sc[h] = m_new
            if do_scores:
                mx_ref[...] = mx

    step(-1, False, True)

    def body(t, carry):
        step(t, True, True)
        return carry

    lax.fori_loop(0, n_items - 1, body, 0)
    step(n_items - 1, True, False)

    @pl.when(j == pl.num_programs(2) - 1)
    def _():
        lv = lam_ref[...]
        lam = (jnp.exp(jnp.sum(lv[0:1] * lv[1:2], axis=1, keepdims=True))
               - jnp.exp(jnp.sum(lv[2:3] * lv[3:4], axis=1, keepdims=True)) + LAMBDA_INIT)
        for h in range(HEADS):
            o = acc_a[h] / l_a[h] - lam * (acc_b[h] / l_b[h])
            o = o * lax.rsqrt(jnp.mean(o * o, axis=0, keepdims=True) + NORM_EPS)
            o = o * sub_ref[...] * (1.0 - LAMBDA_INIT)
            o_ref[:, h * V_DIM:(h + 1) * V_DIM] = o.T.astype(BF16)


def _attention(qz, k, vt, lam_vecs, sub_col, batch, seq, tq, ck, sb, kv_block):
    nq = seq // tq
    nkb = seq // kv_block
    n_chunks = kv_block // ck
    ckv = vt.shape[-1]
    stat = pltpu.VMEM((HEADS, 1, tq), F32)
    kv_mode = dict(pipeline_mode=pl.Buffered(1)) if batch * nkb == 1 else {}
    return pl.pallas_call(
        functools.partial(_attn_kernel, ck=ck, sb=sb, n_chunks=n_chunks),
        grid=(batch, nq, nkb),
        in_specs=[
            pl.BlockSpec((2 * HEADS, tq, LANES), lambda b, i, j: (0, b * nq + i, 0)),
            pl.BlockSpec((HEADS, kv_block, LANES), lambda b, i, j: (0, b * nkb + j, 0), **kv_mode),
            pl.BlockSpec((kv_block // ckv, HEADS, V_DIM, ckv), lambda b, i, j: (b * nkb + j, 0, 0, 0),
                         **kv_mode),
            pl.BlockSpec((4, HEAD_DIM), lambda b, i, j: (0, 0)),
            pl.BlockSpec((V_DIM, 1), lambda b, i, j: (0, 0)),
        ],
        out_specs=pl.BlockSpec((tq, V_WIDTH), lambda b, i, j: (b * nq + i, 0)),
        out_shape=jax.ShapeDtypeStruct((batch * seq, V_WIDTH), BF16),
        scratch_shapes=[
            pltpu.VMEM((ck, tq), F32), pltpu.VMEM((ck, tq), F32),
            pltpu.VMEM((1, tq), F32), pltpu.VMEM((1, tq), F32),
            stat, stat, stat, stat,
            pltpu.VMEM((HEADS, V_DIM, tq), F32), pltpu.VMEM((HEADS, V_DIM, tq), F32),
        ],
        compiler_params=_cparams(("parallel", "parallel", "arbitrary")),
        name="diff_attention",
    )(qz, k, vt, lam_vecs, sub_col)


def _pack_bf16_pairs(a, b):
    ab = lax.bitcast_convert_type(a.astype(BF16).astype(F32), U32)
    bb = lax.bitcast_convert_type(b.astype(BF16).astype(F32), U32)
    return (bb & jnp.uint32(0xFFFF0000)) | (ab >> 16)


def _unpack_bf16_pairs(w):
    lo = lax.bitcast_convert_type(w << 16, F32)
    hi = lax.bitcast_convert_type(w & jnp.uint32(0xFFFF0000), F32)
    return lo, hi


def _merge_kernel(x_ref, four_ref, attn_ref, nmix_ref, wg_ref, bg_ref, wf_ref, wa_ref, wo_ref,
                  nffn_ref, wr_ref, cnt_in_ref,
                  h_ref, hnp_ref, ri_ref, rw_ref, cnt_ref, carry_sc):
    i = pl.program_id(0)

    @pl.when(i == 0)
    def _():
        carry_sc[...] = cnt_in_ref[...]

    x = x_ref[...]
    xn = x * lax.rsqrt(jnp.mean(x * x, axis=-1, keepdims=True) + NORM_EPS) * nmix_ref[...]
    gates = jax.nn.sigmoid(jnp.dot(xn.astype(BF16), wg_ref[...], preferred_element_type=F32)
                           + bg_ref[...])
    bf = jnp.dot(four_ref[...], wf_ref[...], preferred_element_type=F32)
    ba = jnp.dot(attn_ref[...], wa_ref[...], preferred_element_type=F32)
    merged = gates[:, :D_MODEL] * bf + gates[:, D_MODEL:] * ba
    h = x + jnp.dot(merged.astype(BF16), wo_ref[...], preferred_element_type=F32)
    h_ref[...] = h
    hn = h * lax.rsqrt(jnp.mean(h * h, axis=-1, keepdims=True) + NORM_EPS) * nffn_ref[...]
    hnp_ref[...] = _pack_bf16_pairs(hn[:, :D_MODEL // 2], hn[:, D_MODEL // 2:])

    tm = hn.shape[0]
    hn_hi = hn.astype(BF16)
    hn_lo = (hn - hn_hi.astype(F32)).astype(BF16)
    parts = lax.dot_general(jnp.concatenate([hn_hi, hn_lo], axis=0), wr_ref[...],
                            (((1,), (1,)), ((), ())), preferred_element_type=F32)
    lg = (((parts[tm:, ROUTER_LANES:] + parts[:tm, ROUTER_LANES:]) + parts[tm:, :ROUTER_LANES])
          + parts[:tm, :ROUTER_LANES])
    lane = lax.broadcasted_iota(I32, (tm, ROUTER_LANES), 1)
    gmask = lane < N_GROUPS
    gl = jnp.where(gmask, lg, NEG_BIG)
    gmax = jnp.max(gl, axis=1, keepdims=True)
    g_idx = jnp.min(jnp.where(gl == gmax, lane, ROUTER_LANES), axis=1, keepdims=True)
    g_w = 1.0 / jnp.sum(jnp.where(gmask, jnp.exp(gl - gmax), 0.0), axis=1, keepdims=True)
    e_lane = lane - EXPERT_LANE0
    emask = (e_lane >= 0) & (e_lane < N_EXPERTS) & ((e_lane // EXPERTS_PER_GROUP) == g_idx)
    el = jnp.where(emask, lg, NEG_BIG)
    v1 = jnp.max(el, axis=1, keepdims=True)
    i1 = jnp.min(jnp.where(el == v1, lane, ROUTER_LANES), axis=1, keepdims=True)
    el2 = jnp.where(lane == i1, NEG_BIG, el)
    v2 = jnp.max(el2, axis=1, keepdims=True)
    i2 = jnp.min(jnp.where(el2 == v2, lane, ROUTER_LANES), axis=1, keepdims=True)
    s21 = jnp.exp(v2 - v1)
    w1 = g_w / (1.0 + s21)
    w2 = g_w * s21 / (1.0 + s21)

    oh1 = lane == i1
    oh2 = lane == i2
    ohs = (oh1 | oh2).astype(BF16)
    row = lax.broadcasted_iota(I32, (tm, tm), 0)
    col = lax.broadcasted_iota(I32, (tm, tm), 1)
    tril = (col < row).astype(BF16)
    before = jnp.dot(tril, ohs, preferred_element_type=F32) + carry_sc[...]
    r1 = jnp.sum(jnp.where(oh1, before, 0.0), axis=1, keepdims=True)
    r2 = jnp.sum(jnp.where(oh2, before, 0.0), axis=1, keepdims=True)
    carry_sc[...] = carry_sc[...] + jnp.sum(ohs.astype(F32), axis=0, keepdims=True)
    cnt_ref[...] = carry_sc[...]

    e1 = i1 - EXPERT_LANE0
    e2 = i2 - EXPERT_LANE0
    ri = jnp.where(lane == 0, e1.astype(F32), jnp.where(lane == 1, e2.astype(F32),
         jnp.where(lane == 2, r1, jnp.where(lane == 3, r2, 0.0))))
    ri_ref[...] = ri.T
    rw_ref[...] = jnp.where(lane == 0, w1, jnp.where(lane == 1, w2, 0.0))


def _merge(x, four, attn, nmix, wg, bg, wf, wa, wo, nffn, wr, cnt_in, tm):
    te = x.shape[0]
    nt = te // tm
    const = lambda i: (0, 0)
    tile = lambda i: (i, 0)
    return pl.pallas_call(
        _merge_kernel,
        grid=(nt,),
        in_specs=[
            pl.BlockSpec((tm, D_MODEL), tile),
            pl.BlockSpec((tm, FOURIER_WIDTH), tile),
            pl.BlockSpec((tm, V_WIDTH), tile),
            pl.BlockSpec((1, D_MODEL), const),
            pl.BlockSpec((D_MODEL, 2 * D_MODEL), const),
            pl.BlockSpec((1, 2 * D_MODEL), const),
            pl.BlockSpec((FOURIER_WIDTH, D_MODEL), const),
            pl.BlockSpec((V_WIDTH, D_MODEL), const),
            pl.BlockSpec((D_MODEL, D_MODEL), const),
            pl.BlockSpec((1, D_MODEL), const),
            pl.BlockSpec((2 * ROUTER_LANES, D_MODEL), const),
            pl.BlockSpec((1, ROUTER_LANES), const),
        ],
        out_specs=[
            pl.BlockSpec((tm, D_MODEL), tile),
            pl.BlockSpec((tm, D_MODEL // 2), tile),
            pl.BlockSpec((ROUTER_LANES, tm), lambda i: (0, i)),
            pl.BlockSpec((tm, ROUTER_LANES), tile),
            pl.BlockSpec((1, ROUTER_LANES), const),
        ],
        out_shape=[
            jax.ShapeDtypeStruct((te, D_MODEL), F32),
            jax.ShapeDtypeStruct((te, D_MODEL // 2), U32),
            jax.ShapeDtypeStruct((ROUTER_LANES, te), F32),
            jax.ShapeDtypeStruct((te, ROUTER_LANES), F32),
            jax.ShapeDtypeStruct((1, ROUTER_LANES), F32),
        ],
        scratch_shapes=[pltpu.VMEM((1, ROUTER_LANES), F32)],
        compiler_params=_cparams(("arbitrary",)),
        name="merge_router",
    )(x, four, attn, nmix, wg, bg, wf, wa, wo, nffn, wr, cnt_in)


def _tile_dest(dest, tm):
    nt = dest.shape[1] // tm
    return dest.reshape(2, nt, tm).transpose(1, 0, 2).reshape(nt, 1, 2 * tm)


def _dispatch_kernel(zs_ref, zl_ref, dest_ref, *rest, tm, tiles):
    n_src = len(tiles)
    src_refs = rest[:n_src]
    xs_ref, zero_sc, sem, zsem = rest[n_src:]
    i = pl.program_id(0)

    def copy_tile(src_ref):
        def row_copy(t, slot):
            return pltpu.make_async_copy(src_ref.at[pl.ds(t, 1)],
                                         xs_ref.at[pl.ds(dest_ref[0, 0, slot * tm + t], 1)], sem)

        def start(t, c):
            row_copy(t, 0).start(priority=0)
            row_copy(t, 1).start(priority=1)
            return c

        lax.fori_loop(0, tm, start, 0, unroll=8)

        def wait(t, c):
            row_copy(t, 0).wait()
            row_copy(t, 1).wait()
            return c

        lax.fori_loop(0, tm, wait, 0, unroll=8)

    first_tile = 0
    for src_ref, n_tiles in zip(src_refs, tiles):
        @pl.when((i >= first_tile) & (i < first_tile + n_tiles))
        def _(src_ref=src_ref):
            copy_tile(src_ref)
        first_tile += n_tiles

    @pl.when(i == 0)
    def _():
        zero_sc[...] = jnp.zeros(zero_sc.shape, U32)

        def zero_copy(r):
            return pltpu.make_async_copy(zero_sc.at[pl.ds(0, 1)], xs_ref.at[pl.ds(r, 1)], zsem)

        def per_expert(e, c):
            def zstart(r, c2):
                zero_copy(zs_ref[e] + r).start()
                return c2
            lax.fori_loop(0, zl_ref[e], zstart, 0)

            def zwait(r, c2):
                zero_copy(zs_ref[e] + r).wait()
                return c2
            lax.fori_loop(0, zl_ref[e], zwait, 0)
            return c
        lax.fori_loop(0, N_EXPERTS, per_expert, 0)

        def block_copy(b):
            return pltpu.make_async_copy(
                zero_sc, xs_ref.at[pl.ds(pl.multiple_of(b * MOE_ROWS, MOE_ROWS), MOE_ROWS)], zsem)

        n_blocks = xs_ref.shape[0] // MOE_ROWS

        def bstart(b, c):
            block_copy(b).start()
            return c
        lax.fori_loop(zl_ref[N_EXPERTS], n_blocks, bstart, 0)

        def bwait(b, c):
            block_copy(b).wait()
            return c
        lax.fori_loop(zl_ref[N_EXPERTS], n_blocks, bwait, 0)


def _dispatch(zstart, zlen, dests, srcs, n_rows, tm):
    tiles = tuple(s.shape[0] // tm for s in srcs)
    dest3 = jnp.concatenate([_tile_dest(d, tm) for d in dests], axis=0)
    src_specs = []
    first_tile = 0
    for n in tiles:
        src_specs.append(pl.BlockSpec(
            (tm, D_MODEL // 2), lambda i, zs, zl, ft=first_tile, n=n: (jnp.clip(i - ft, 0, n - 1), 0)))
        first_tile += n
    grid_spec = pltpu.PrefetchScalarGridSpec(
        num_scalar_prefetch=2,
        grid=(sum(tiles),),
        in_specs=[pl.BlockSpec((1, 1, 2 * tm), lambda i, zs, zl: (i, 0, 0), memory_space=pltpu.SMEM)]
        + src_specs,
        out_specs=pl.BlockSpec(memory_space=pl.ANY),
        scratch_shapes=[pltpu.VMEM((MOE_ROWS, D_MODEL // 2), U32), pltpu.SemaphoreType.DMA,
                        pltpu.SemaphoreType.DMA],
    )
    return pl.pallas_call(
        functools.partial(_dispatch_kernel, tm=tm, tiles=tiles),
        grid_spec=grid_spec,
        out_shape=jax.ShapeDtypeStruct((n_rows, D_MODEL // 2), U32),
        compiler_params=_cparams(("arbitrary",)),
        name="moe_dispatch",
    )(zstart, zlen, dest3, *srcs)


def _expert_kernel(be_ref, nu_ref, ord_ref, nxt_ref, xs_ref, wg_hbm, wu_hbm, wd_hbm, o_ref,
                   wbuf, wgb, wub, wdb, sem):
    b = pl.program_id(0)
    used = b < nu_ref[0]
    e = be_ref[b]
    new_expert = (b == 0) | (e != be_ref[jnp.maximum(b - 1, 0)])
    slot = ord_ref[b] & 1

    def fetch(expert, s):
        return [pltpu.make_async_copy(w.at[expert], wbuf.at[s, i], sem.at[s])
                for i, w in enumerate((wg_hbm, wu_hbm, wd_hbm))]

    @pl.when((b == 0) & used)
    def _():
        for cp in fetch(e, 0):
            cp.start()

    @pl.when(used & new_expert)
    def _():
        for cp in fetch(e, slot):
            cp.wait()

        @pl.when(nxt_ref[b] >= 0)
        def _():
            for cp in fetch(nxt_ref[b], 1 - slot):
                cp.start()

        rows = 128

        def cast(r, c):
            r0 = pl.multiple_of(r * rows, rows)
            wgb[pl.ds(r0, rows), :] = wbuf[slot, 0, pl.ds(r0, rows), :].astype(BF16)
            wub[pl.ds(r0, rows), :] = wbuf[slot, 1, pl.ds(r0, rows), :].astype(BF16)
            wdb[pl.ds(r0, rows), :] = wbuf[slot, 2, pl.ds(r0, rows), :].astype(BF16)
            return c
        lax.fori_loop(0, D_MODEL // rows, cast, 0)

    @pl.when(used)
    def _():
        lo, hi = _unpack_bf16_pairs(xs_ref[...])
        x = jnp.concatenate([lo, hi], axis=1).astype(BF16)
        g = jnp.dot(x, wgb[...], preferred_element_type=F32)
        u = jnp.dot(x, wub[...], preferred_element_type=F32)
        hmid = (g * jax.nn.sigmoid(g) * u).astype(BF16)
        y = jnp.dot(hmid, wdb[...], preferred_element_type=F32)
        o_ref[...] = _pack_bf16_pairs(y[:, :D_MODEL // 2], y[:, D_MODEL // 2:])

    @pl.when(jnp.logical_not(used))
    def _():
        o_ref[...] = jnp.zeros(o_ref.shape, U32)


def _experts(block_expert, n_used, block_ord, block_next, xs, wg, wu, wd):
    assert EXPERT_FF == D_MODEL
    n_blocks = xs.shape[0] // MOE_ROWS
    row_spec = pl.BlockSpec((MOE_ROWS, D_MODEL // 2), lambda b, *_: (b, 0))
    grid_spec = pltpu.PrefetchScalarGridSpec(
        num_scalar_prefetch=4,
        grid=(n_blocks,),
        in_specs=[row_spec] + [pl.BlockSpec(memory_space=pl.ANY)] * 3,
        out_specs=row_spec,
        scratch_shapes=[pltpu.VMEM((2, 3, D_MODEL, EXPERT_FF), F32),
                        pltpu.VMEM((D_MODEL, EXPERT_FF), BF16), pltpu.VMEM((D_MODEL, EXPERT_FF), BF16),
                        pltpu.VMEM((EXPERT_FF, D_MODEL), BF16), pltpu.SemaphoreType.DMA((2,))],
    )
    return pl.pallas_call(
        _expert_kernel,
        grid_spec=grid_spec,
        out_shape=jax.ShapeDtypeStruct((xs.shape[0], D_MODEL // 2), U32),
        compiler_params=_cparams(("arbitrary",)),
        name="moe_experts",
    )(block_expert, n_used, block_ord, block_next, xs, wg, wu, wd)


def _combine_kernel(dest_ref, ys_ref, h_ref, rw_ref, nw_ref, o_ref, buf, sem, *, tm):
    def row_copy(t, slot):
        return pltpu.make_async_copy(ys_ref.at[pl.ds(dest_ref[0, 0, slot * tm + t], 1)],
                                     buf.at[slot, pl.ds(t, 1)], sem)

    def start(t, c):
        row_copy(t, 0).start(priority=0)
        row_copy(t, 1).start(priority=1)
        return c

    lax.fori_loop(0, tm, start, 0, unroll=8)

    def wait(t, c):
        row_copy(t, 0).wait()
        row_copy(t, 1).wait()
        return c

    lax.fori_loop(0, tm, wait, 0, unroll=8)

    rw = rw_ref[...]
    w1 = rw[:, 0:1]
    w2 = rw[:, 1:2]
    lo1, hi1 = _unpack_bf16_pairs(buf[0])
    lo2, hi2 = _unpack_bf16_pairs(buf[1])
    y = jnp.concatenate([lo1 * w1 + lo2 * w2, hi1 * w1 + hi2 * w2], axis=1)
    h = h_ref[...] + y
    o_ref[...] = h * lax.rsqrt(jnp.mean(h * h, axis=-1, keepdims=True) + NORM_EPS) * nw_ref[...]


def _combine(dest, ys, h, rw, nw, tm):
    te = h.shape[0]
    nt = te // tm
    dest3 = _tile_dest(dest, tm)
    return pl.pallas_call(
        functools.partial(_combine_kernel, tm=tm),
        grid=(nt,),
        in_specs=[
            pl.BlockSpec((1, 1, 2 * tm), lambda i: (i, 0, 0), memory_space=pltpu.SMEM),
            pl.BlockSpec(memory_space=pl.ANY),
            pl.BlockSpec((tm, D_MODEL), lambda i: (i, 0)),
            pl.BlockSpec((tm, ROUTER_LANES), lambda i: (i, 0)),
            pl.BlockSpec((1, D_MODEL), lambda i: (0, 0)),
        ],
        out_specs=pl.BlockSpec((tm, D_MODEL), lambda i: (i, 0)),
        out_shape=jax.ShapeDtypeStruct((te, D_MODEL), F32),
        scratch_shapes=[pltpu.VMEM((2, tm, D_MODEL // 2), U32), pltpu.SemaphoreType.DMA],
        compiler_params=_cparams(("arbitrary",)),
        name="moe_combine",
    )(dest3, ys, h, rw, nw)


def _split_router_weight(w_group, w_expert):
    wrt = jnp.concatenate([w_group.T, w_expert.T], axis=0)
    wrt = jnp.pad(wrt, ((0, ROUTER_LANES - wrt.shape[0]), (0, 0)))
    hi = wrt.astype(BF16)
    lo = (wrt - hi.astype(F32)).astype(BF16)
    return jnp.concatenate([hi, lo], axis=0)


def _rope_tables(seq):
    pos = jnp.arange(seq, dtype=F32)
    inv_freq = ROPE_THETA ** (-jnp.arange(0, HEAD_DIM, 2, dtype=F32) / HEAD_DIM)
    ang = pos[:, None] * inv_freq[None, :]
    c, s = jnp.cos(ang), jnp.sin(ang)
    cos = jnp.concatenate([c, c, c, c], axis=1)
    sin = jnp.concatenate([-s, s, -s, s], axis=1)
    return cos, sin


def _token_mixer(x2d, batch, seq, p, cnt_in):
    cos, sin = _rope_tables(seq)
    f, qz, k, vt = _inproj(x2d, p["nmix"], p["w1"], cos, sin, tm=512)
    four = _fourier(f, batch, seq)
    attn = _attention(qz, k, vt, p["lam"], p["sub"], batch, seq, tq=512, ck=min(2048, seq // 2),
                      sb=256, kv_block=seq)
    return _merge(x2d, four, attn, p["nmix"], p["wgate"], p["bg"], p["wf"], p["wa"], p["wo"],
                  p["nffn"], p["wr"], cnt_in, tm=512)


def kernel(x_prompt, x_sample, norm_mix_w, w_in, b_gate, w_fourier, lambda_q1, lambda_k1, lambda_q2,
           lambda_k2, subln_w, w_attn, w_out, norm_ffn_w, w_group_router, w_expert_router,
           w_expert_gate, w_expert_up, w_expert_down, norm_final_w):
    wr = _split_router_weight(w_group_router[0], w_expert_router[0])
    p = {
        "nmix": norm_mix_w[0][None, :],
        "w1": w_in[0][:, :2048].astype(BF16),
        "wgate": w_in[0][:, 2048:].astype(BF16),
        "bg": b_gate[0][None, :],
        "wf": w_fourier[0].astype(BF16),
        "wa": w_attn[0].astype(BF16),
        "wo": w_out[0].astype(BF16),
        "nffn": norm_ffn_w[0][None, :],
        "wr": wr,
        "lam": jnp.stack([lambda_q1[0], lambda_k1[0], lambda_q2[0], lambda_k2[0]]),
        "sub": subln_w[0][:, None],
    }
    inputs = [x_prompt, x_sample]
    mixed = []
    cnt = jnp.zeros((1, ROUTER_LANES), F32)
    for x in inputs:
        batch, seq, _ = x.shape
        h, hnp, ri, rw, cnt = _token_mixer(x.reshape(batch * seq, D_MODEL), batch, seq, p, cnt)
        mixed.append((h, hnp, ri, rw))
    outs = _moe_and_final(mixed, cnt, w_expert_gate[0], w_expert_up[0], w_expert_down[0], norm_final_w)
    return tuple(o.reshape(x.shape) for o, x in zip(outs, inputs))


def _moe_and_final(mixed, cnt, w_gate, w_up, w_down, norm_final_w):
    n_assign = 2 * sum(m[0].shape[0] for m in mixed)
    n_blocks = -(-(n_assign + N_EXPERTS * (MOE_ROWS - 1)) // MOE_ROWS)
    n_rows = n_blocks * MOE_ROWS
    counts = cnt[0, EXPERT_LANE0:EXPERT_LANE0 + N_EXPERTS].astype(I32)
    padded = (counts + MOE_ROWS - 1) // MOE_ROWS * MOE_ROWS
    pend = jnp.cumsum(padded)
    pstart = pend - padded
    total = pend[-1]
    bstart = jnp.arange(n_blocks, dtype=I32) * MOE_ROWS
    be = jnp.minimum(jnp.sum((pend[None, :] <= bstart[:, None]).astype(I32), axis=1), N_EXPERTS - 1)
    n_used = (total // MOE_ROWS).astype(I32)
    be = jnp.where(bstart < total, be, be[jnp.maximum(n_used - 1, 0)])
    block_ord = jnp.cumsum(jnp.concatenate([jnp.zeros((1,), I32), (be[1:] != be[:-1]).astype(I32)]))
    eidx = jnp.arange(N_EXPERTS, dtype=I32)
    later_used = (eidx[None, :] > eidx[:, None]) & (padded[None, :] > 0)
    next_expert = jnp.min(jnp.where(later_used, eidx[None, :], N_EXPERTS), axis=1)
    next_expert = jnp.where(next_expert == N_EXPERTS, -1, next_expert)
    block_next = next_expert[be]
    zstart = pstart + counts
    zlen = jnp.concatenate([(pend - zstart).astype(I32), n_used[None]])

    def row_of(expert, rank):
        hit = expert[None].astype(I32) == eidx[:, None, None]
        return jnp.sum(jnp.where(hit, pstart[:, None, None], 0), axis=0) + rank.astype(I32)

    dests = [row_of(m[2][0:2], m[2][2:4]) for m in mixed]
    xs = _dispatch(zstart, zlen, dests, [m[1] for m in mixed], n_rows, tm=1024)
    ys = _experts(be, n_used[None], block_ord, block_next, xs, w_gate, w_up, w_down)
    return [_combine(dest, ys, m[0], m[3], norm_final_w[None, :], tm=512)
            for m, dest in zip(mixed, dests)]
```

```python
import functools
import math

import numpy as np
import jax
import jax.numpy as jnp
from jax import lax
from jax.experimental import pallas as pl
from jax.experimental.pallas import tpu as pltpu

F32 = jnp.float32
BF16 = jnp.bfloat16
U32 = jnp.uint32
I32 = jnp.int32

D_MODEL = 1024
FOURIER_WIDTH = 512
FOURIER_GROUPS = 4
GROUP_DIM = 128
HEADS = 4
HEAD_DIM = 64
QK_WIDTH = 512
V_DIM = 128
V_WIDTH = 512
ROPE_THETA = 10000.0
N_GROUPS = 4
EXPERTS_PER_GROUP = 8
N_EXPERTS = 32
EXPERT_FF = 1024
NORM_EPS = 1e-6
LAMBDA_INIT = 0.8 - 0.6 * math.exp(0.0)

LANES = 128
ROUTER_LANES = 128
EXPERT_LANE0 = N_GROUPS
MOE_ROWS = 256
ROW_GROUP = 8
DFT_N = 128
VMEM_LIMIT = 56 * 1024 * 1024

LOG2E = 1.4426950408889634
NEG_BIG = -3.0e38


def _cparams(sem, vmem=VMEM_LIMIT, flags=None):
    return pltpu.CompilerParams(dimension_semantics=sem, vmem_limit_bytes=vmem, flags=flags)


def _swap_halves(t):
    n = t.shape[1]
    lane = lax.broadcasted_iota(I32, t.shape, 1)
    first = (lane % HEAD_DIM) < (HEAD_DIM // 2)
    return jnp.where(first, pltpu.roll(t, n - HEAD_DIM // 2, 1), pltpu.roll(t, HEAD_DIM // 2, 1))


def _inproj_kernel(x_ref, nw_ref, w_ref, cos_ref, sin_ref, f_ref, qz_ref, k_ref, vt_ref):
    x = x_ref[...]
    xn = x * lax.rsqrt(jnp.mean(x * x, axis=-1, keepdims=True) + NORM_EPS) * nw_ref[...]
    proj = jnp.dot(xn.astype(BF16), w_ref[...], preferred_element_type=F32)
    f_ref[...] = proj[:, :FOURIER_WIDTH].astype(BF16)
    q = proj[:, FOURIER_WIDTH:FOURIER_WIDTH + QK_WIDTH]
    k = proj[:, FOURIER_WIDTH + QK_WIDTH:FOURIER_WIDTH + 2 * QK_WIDTH]
    v = proj[:, FOURIER_WIDTH + 2 * QK_WIDTH:]
    cos = jnp.concatenate([cos_ref[...]] * (QK_WIDTH // LANES), axis=1)
    sin = jnp.concatenate([sin_ref[...]] * (QK_WIDTH // LANES), axis=1)
    q = q * cos + _swap_halves(q) * sin
    k = k * cos + _swap_halves(k) * sin
    q = q * (HEAD_DIM ** -0.5 * LOG2E)
    lane = lax.broadcasted_iota(I32, (q.shape[0], LANES), 1)
    lo = lane < HEAD_DIM
    vt = v.T.astype(BF16)
    for h in range(HEADS):
        qh = q[:, h * LANES:(h + 1) * LANES]
        qz_ref[2 * h] = jnp.where(lo, qh, 0.0).astype(BF16)
        qz_ref[2 * h + 1] = jnp.where(lo, 0.0, qh).astype(BF16)
        k_ref[h] = k[:, h * LANES:(h + 1) * LANES].astype(BF16)
        vt_ref[0, h] = vt[h * V_DIM:(h + 1) * V_DIM, :]


def _inproj(x, nw, w1, cos, sin, tm):
    te = x.shape[0]
    nt = te // tm
    pos_tiles = cos.shape[0] // tm
    return pl.pallas_call(
        _inproj_kernel,
        grid=(nt,),
        in_specs=[
            pl.BlockSpec((tm, D_MODEL), lambda i: (i, 0)),
            pl.BlockSpec((1, D_MODEL), lambda i: (0, 0)),
            pl.BlockSpec((D_MODEL, 2048), lambda i: (0, 0)),
            pl.BlockSpec((tm, LANES), lambda i: (i % pos_tiles, 0)),
            pl.BlockSpec((tm, LANES), lambda i: (i % pos_tiles, 0)),
        ],
        out_specs=[
            pl.BlockSpec((tm, FOURIER_WIDTH), lambda i: (i, 0)),
            pl.BlockSpec((2 * HEADS, tm, LANES), lambda i: (0, i, 0)),
            pl.BlockSpec((HEADS, tm, LANES), lambda i: (0, i, 0)),
            pl.BlockSpec((1, HEADS, V_DIM, tm), lambda i: (i, 0, 0, 0)),
        ],
        out_shape=[
            jax.ShapeDtypeStruct((te, FOURIER_WIDTH), BF16),
            jax.ShapeDtypeStruct((2 * HEADS, te, LANES), BF16),
            jax.ShapeDtypeStruct((HEADS, te, LANES), BF16),
            jax.ShapeDtypeStruct((nt, HEADS, V_DIM, tm), BF16),
        ],
        compiler_params=_cparams(("parallel",)),
        name="inproj",
    )(x, nw, w1, cos, sin)


@functools.lru_cache(maxsize=None)
def _dft_tables(batch, seq):
    n1 = seq // DFT_N
    assert batch * n1 == DFT_N
    idx = np.arange(n1)
    ang = 2.0 * np.pi * ((idx[:, None] * idx[None, :]) % n1) / n1
    eye = np.eye(batch)
    m1 = np.concatenate([np.kron(eye, np.cos(ang)), -np.kron(eye, np.sin(ang))], axis=0)
    k1 = np.arange(n1)[:, None, None]
    k2 = np.arange(DFT_N)[None, :, None]
    n2 = np.arange(DFT_N)[None, None, :]
    ang2 = 2.0 * np.pi * ((n2 * (k1 + n1 * k2)) % seq) / seq
    er, ei = np.cos(ang2), -np.sin(ang2)
    m2 = np.concatenate([np.concatenate([er, -ei], axis=2), np.concatenate([ei, er], axis=2)], axis=1)
    c = np.arange(GROUP_DIM)
    ang3 = 2.0 * np.pi * ((c[:, None] * c[None, :]) % GROUP_DIM) / GROUP_DIM
    norm = 1.0 / math.sqrt(seq * GROUP_DIM)
    cs = np.concatenate([np.cos(ang3), np.sin(ang3)], axis=0) * norm
    return (jnp.asarray(m1, BF16), jnp.asarray(m2, BF16), jnp.asarray(cs, BF16))


def _dft1_kernel(m_ref, x_ref, y_ref):
    y_ref[...] = jnp.dot(m_ref[...], x_ref[...], preferred_element_type=F32).astype(BF16)


def _dft1(m1, x2d, tn=4096):
    n = x2d.shape[1]
    return pl.pallas_call(
        _dft1_kernel,
        grid=(n // tn,),
        in_specs=[pl.BlockSpec((2 * DFT_N, DFT_N), lambda i: (0, 0)),
                  pl.BlockSpec((DFT_N, tn), lambda i: (0, i))],
        out_specs=pl.BlockSpec((2 * DFT_N, tn), lambda i: (0, i)),
        out_shape=jax.ShapeDtypeStruct((2 * DFT_N, n), BF16),
        compiler_params=_cparams(("parallel",)),
        name="dft_stage1",
    )(m1, x2d)


def _dft2_kernel(m_ref, yr_ref, yi_ref, cs_ref, o_ref, *, rows_per_step):
    for j in range(rows_per_step):
        m = m_ref[j]
        z = (jnp.dot(m[:, :DFT_N], yr_ref[0, j], preferred_element_type=F32)
             + jnp.dot(m[:, DFT_N:], yi_ref[0, j], preferred_element_type=F32))
        zb = z.astype(BF16)
        for g in range(FOURIER_GROUPS):
            zg = jnp.concatenate([zb[:DFT_N, g * LANES:(g + 1) * LANES],
                                  zb[DFT_N:, g * LANES:(g + 1) * LANES]], axis=1)
            o_ref[0, :, (j * FOURIER_GROUPS + g) * LANES:(j * FOURIER_GROUPS + g + 1) * LANES] = (
                jnp.dot(zg, cs_ref[...], preferred_element_type=F32).astype(BF16))


def _dft2(m2, y, cs, batch, seq, rows_per_step=4):
    n1 = seq // DFT_N
    r = rows_per_step
    steps = DFT_N // r
    per_b = n1 // r
    return pl.pallas_call(
        functools.partial(_dft2_kernel, rows_per_step=r),
        grid=(steps,),
        in_specs=[
            pl.BlockSpec((r, 2 * DFT_N, 2 * DFT_N), lambda i: (i % per_b, 0, 0)),
            pl.BlockSpec((1, r, DFT_N, FOURIER_WIDTH), lambda i: (0, i, 0, 0)),
            pl.BlockSpec((1, r, DFT_N, FOURIER_WIDTH), lambda i: (1, i, 0, 0)),
            pl.BlockSpec((2 * GROUP_DIM, GROUP_DIM), lambda i: (0, 0)),
        ],
        out_specs=pl.BlockSpec((1, DFT_N, r * FOURIER_WIDTH), lambda i: (i // per_b, 0, i % per_b)),
        out_shape=jax.ShapeDtypeStruct((batch, DFT_N, n1 * FOURIER_WIDTH), BF16),
        compiler_params=_cparams(("parallel",)),
        name="dft_stage2",
    )(m2, y, y, cs)


def _fourier(f, batch, seq):
    m1, m2, cs = _dft_tables(batch, seq)
    y = _dft1(m1, f.reshape(DFT_N, DFT_N * FOURIER_WIDTH))
    out = _dft2(m2, y.reshape(2, DFT_N, DFT_N, FOURIER_WIDTH), cs, batch, seq)
    return out.reshape(batch * seq, FOURIER_WIDTH)


def _attn_kernel(qz_ref, k_ref, vt_ref, lam_ref, sub_ref, o_ref,
                 s_a, s_b, mx_a, mx_b, m_a, m_b, l_a, l_b, acc_a, acc_b, *, ck, sb, n_chunks):
    j = pl.program_id(2)
    n_items = n_chunks * HEADS

    @pl.when(j == 0)
    def _():
        for m_sc, l_sc, acc_sc in ((m_a, l_a, acc_a), (m_b, l_b, acc_b)):
            m_sc[...] = jnp.full(m_sc.shape, NEG_BIG, F32)
            l_sc[...] = jnp.zeros(l_sc.shape, F32)
            acc_sc[...] = jnp.zeros(acc_sc.shape, F32)

    def split(t):
        t = jnp.asarray(t, I32)
        return lax.shift_right_logical(t, 2), t & (HEADS - 1)

    ckv = vt_ref.shape[-1]
    n_sub = ck // sb
    bufs = ((s_a, mx_a, m_a, l_a, acc_a), (s_b, mx_b, m_b, l_b, acc_b))

    def step(t, do_update, do_scores):
        if do_update:
            c, h = split(t)
        if do_scores:
            c1, h1 = split(t + 1)
        state = []
        for cc, (s_ref, mx_ref, m_sc, l_sc, acc_sc) in enumerate(bufs):
            if do_update:
                m_old = m_sc[h]
                m_new = jnp.maximum(m_old, mx_ref[...])
                state.append([m_old, m_new, None, None, None])
            else:
                state.append([None, None, None, None, None])
        for u in range(n_sub):
            rows = pl.ds(u * sb, sb)
            for cc, (s_ref, mx_ref, m_sc, l_sc, acc_sc) in enumerate(bufs):
                st = state[cc]
                if do_update:
                    p = jnp.exp2(s_ref[rows, :] - st[1])
                    ps = jnp.sum(p, axis=0, keepdims=True)
                    lane0 = (u * sb) % ckv
                    vth = vt_ref[c * (ck // ckv) + (u * sb) // ckv, h, :, lane0:lane0 + sb]
                    d = jnp.dot(vth, p.astype(BF16), preferred_element_type=F32)
                    st[2] = ps if st[2] is None else st[2] + ps
                    st[3] = d if st[3] is None else st[3] + d
                if do_scores:
                    row0 = pl.multiple_of(c1 * ck + u * sb, sb)
                    s = lax.dot_general(k_ref[h1, pl.ds(row0, sb), :], qz_ref[2 * h1 + cc],
                                        (((1,), (1,)), ((), ())), preferred_element_type=F32)
                    s_ref[rows, :] = s
                    mx = jnp.max(s, axis=0, keepdims=True)
                    st[4] = mx if st[4] is None else jnp.maximum(st[4], mx)
        for cc, (s_ref, mx_ref, m_sc, l_sc, acc_sc) in enumerate(bufs):
            m_old, m_new, l_add, pv, mx = state[cc]
            if do_update:
                alpha = jnp.exp2(m_old - m_new)
                l_sc[h] = alpha * l_sc[h] + l_add
                acc_sc[h] = alpha * acc_sc[h] + pv
                m_sc[h] = m_new
            if do_scores:
                mx_ref[...] = mx

    step(-1, False, True)

    def body(t, carry):
        step(t, True, True)
        return carry

    lax.fori_loop(0, n_items - 1, body, 0)
    step(n_items - 1, True, False)

    @pl.when(j == pl.num_programs(2) - 1)
    def _():
        lv = lam_ref[...]
        lam = (jnp.exp(jnp.sum(lv[0:1] * lv[1:2], axis=1, keepdims=True))
               - jnp.exp(jnp.sum(lv[2:3] * lv[3:4], axis=1, keepdims=True)) + LAMBDA_INIT)
        for h in range(HEADS):
            o = acc_a[h] / l_a[h] - lam * (acc_b[h] / l_b[h])
            o = o * lax.rsqrt(jnp.mean(o * o, axis=0, keepdims=True) + NORM_EPS)
            o = o * sub_ref[...] * (1.0 - LAMBDA_INIT)
            o_ref[:, h * V_DIM:(h + 1) * V_DIM] = o.T.astype(BF16)


def _attention(qz, k, vt, lam_vecs, sub_col, batch, seq, tq, ck, sb, kv_block):
    nq = seq // tq
    nkb = seq // kv_block
    n_chunks = kv_block // ck
    ckv = vt.shape[-1]
    stat = pltpu.VMEM((HEADS, 1, tq), F32)
    kv_mode = dict(pipeline_mode=pl.Buffered(1)) if batch * nkb == 1 else {}
    return pl.pallas_call(
        functools.partial(_attn_kernel, ck=ck, sb=sb, n_chunks=n_chunks),
        grid=(batch, nq, nkb),
        in_specs=[
            pl.BlockSpec((2 * HEADS, tq, LANES), lambda b, i, j: (0, b * nq + i, 0)),
            pl.BlockSpec((HEADS, kv_block, LANES), lambda b, i, j: (0, b * nkb + j, 0), **kv_mode),
            pl.BlockSpec((kv_block // ckv, HEADS, V_DIM, ckv), lambda b, i, j: (b * nkb + j, 0, 0, 0),
                         **kv_mode),
            pl.BlockSpec((4, HEAD_DIM), lambda b, i, j: (0, 0)),
            pl.BlockSpec((V_DIM, 1), lambda b, i, j: (0, 0)),
        ],
        out_specs=pl.BlockSpec((tq, V_WIDTH), lambda b, i, j: (b * nq + i, 0)),
        out_shape=jax.ShapeDtypeStruct((batch * seq, V_WIDTH), BF16),
        scratch_shapes=[
            pltpu.VMEM((ck, tq), F32), pltpu.VMEM((ck, tq), F32),
            pltpu.VMEM((1, tq), F32), pltpu.VMEM((1, tq), F32),
            stat, stat, stat, stat,
            pltpu.VMEM((HEADS, V_DIM, tq), F32), pltpu.VMEM((HEADS, V_DIM, tq), F32),
        ],
        compiler_params=_cparams(("parallel", "parallel", "arbitrary")),
        name="diff_attention",
    )(qz, k, vt, lam_vecs, sub_col)


def _pack_bf16_pairs(a, b):
    ab = lax.bitcast_convert_type(a.astype(BF16).astype(F32), U32)
    bb = lax.bitcast_convert_type(b.astype(BF16).astype(F32), U32)
    return (bb & jnp.uint32(0xFFFF0000)) | (ab >> 16)


def _unpack_bf16_pairs(w):
    lo = lax.bitcast_convert_type(w << 16, F32)
    hi = lax.bitcast_convert_type(w & jnp.uint32(0xFFFF0000), F32)
    return lo, hi


def _merge_kernel(x_ref, four_ref, attn_ref, nmix_ref, wg_ref, bg_ref, wf_ref, wa_ref, wo_ref,
                  nffn_ref, wr_ref, cnt_in_ref,
                  h_ref, hnp_ref, ri_ref, rw_ref, cnt_ref, carry_sc, hn_sc):
    i = pl.program_id(0)

    @pl.when(i == 0)
    def _():
        carry_sc[...] = cnt_in_ref[...]
        hn_sc[...] = jnp.zeros(hn_sc.shape, F32)

    st = {}
    tm = hn_sc.shape[0]
    lane = lax.broadcasted_iota(I32, (tm, ROUTER_LANES), 1)

    def r_logits():
        hn = hn_sc[...]
        hn_hi = hn.astype(BF16)
        hn_lo = (hn - hn_hi.astype(F32)).astype(BF16)
        parts = lax.dot_general(jnp.concatenate([hn_hi, hn_lo], axis=0), wr_ref[...],
                                (((1,), (1,)), ((), ())), preferred_element_type=F32)
        st["lg"] = (((parts[tm:, ROUTER_LANES:] + parts[:tm, ROUTER_LANES:])
                     + parts[tm:, :ROUTER_LANES]) + parts[:tm, :ROUTER_LANES])

    def r_group():
        gmask = lane < N_GROUPS
        gl = jnp.where(gmask, st["lg"], NEG_BIG)
        gmax = jnp.max(gl, axis=1, keepdims=True)
        st["g_idx"] = jnp.min(jnp.where(gl == gmax, lane, ROUTER_LANES), axis=1, keepdims=True)
        st["g_w"] = 1.0 / jnp.sum(jnp.where(gmask, jnp.exp(gl - gmax), 0.0), axis=1, keepdims=True)

    def r_top1():
        e_lane = lane - EXPERT_LANE0
        emask = ((e_lane >= 0) & (e_lane < N_EXPERTS)
                 & ((e_lane // EXPERTS_PER_GROUP) == st["g_idx"]))
        el = jnp.where(emask, st["lg"], NEG_BIG)
        st["v1"] = jnp.max(el, axis=1, keepdims=True)
        st["i1"] = jnp.min(jnp.where(el == st["v1"], lane, ROUTER_LANES), axis=1, keepdims=True)
        st["el"] = el

    def r_top2():
        el2 = jnp.where(lane == st["i1"], NEG_BIG, st["el"])
        v2 = jnp.max(el2, axis=1, keepdims=True)
        st["i2"] = jnp.min(jnp.where(el2 == v2, lane, ROUTER_LANES), axis=1, keepdims=True)
        s21 = jnp.exp(v2 - st["v1"])
        w1 = st["g_w"] / (1.0 + s21)
        w2 = st["g_w"] * s21 / (1.0 + s21)
        rw_ref[...] = jnp.where(lane == 0, w1, jnp.where(lane == 1, w2, 0.0))

    def r_rank():
        oh1 = lane == st["i1"]
        oh2 = lane == st["i2"]
        ohs = (oh1 | oh2).astype(BF16)
        row = lax.broadcasted_iota(I32, (tm, tm), 0)
        col = lax.broadcasted_iota(I32, (tm, tm), 1)
        tril = (col < row).astype(BF16)
        before = jnp.dot(tril, ohs, preferred_element_type=F32) + carry_sc[...]
        st["r1"] = jnp.sum(jnp.where(oh1, before, 0.0), axis=1, keepdims=True)
        st["r2"] = jnp.sum(jnp.where(oh2, before, 0.0), axis=1, keepdims=True)
        real_tile = (i > 0).astype(F32)
        carry_sc[...] = carry_sc[...] + real_tile * jnp.sum(ohs.astype(F32), axis=0, keepdims=True)
        cnt_ref[...] = carry_sc[...]

    def r_store():
        e1 = (st["i1"] - EXPERT_LANE0).astype(F32)
        e2 = (st["i2"] - EXPERT_LANE0).astype(F32)
        ri = jnp.where(lane == 0, e1, jnp.where(lane == 1, e2,
             jnp.where(lane == 2, st["r1"], jnp.where(lane == 3, st["r2"], 0.0))))
        ri_ref[...] = ri.T

    gate_cols = 512

    def m_norm():
        x = x_ref[...]
        xn = x * lax.rsqrt(jnp.mean(x * x, axis=-1, keepdims=True) + NORM_EPS) * nmix_ref[...]
        st["xnb"] = xn.astype(BF16)
        st["gates"] = []

    def m_gate(c):
        cols = slice(c * gate_cols, (c + 1) * gate_cols)
        st["gates"].append(jax.nn.sigmoid(
            jnp.dot(st["xnb"], wg_ref[:, cols], preferred_element_type=F32) + bg_ref[:, cols]))

    def m_fourier():
        st["bf"] = jnp.dot(four_ref[...], wf_ref[...], preferred_element_type=F32)

    def m_attn():
        st["ba"] = jnp.dot(attn_ref[...], wa_ref[...], preferred_element_type=F32)

    def m_merge():
        gates = jnp.concatenate(st["gates"], axis=1)
        st["merged"] = (gates[:, :D_MODEL] * st["bf"] + gates[:, D_MODEL:] * st["ba"]).astype(BF16)

    def m_out():
        h = x_ref[...] + jnp.dot(st["merged"], wo_ref[...], preferred_element_type=F32)
        h_ref[...] = h
        hn_next = h * lax.rsqrt(jnp.mean(h * h, axis=-1, keepdims=True) + NORM_EPS) * nffn_ref[...]
        hnp_ref[...] = _pack_bf16_pairs(hn_next[:, :D_MODEL // 2], hn_next[:, D_MODEL // 2:])
        hn_sc[...] = hn_next

    n_gate = 2 * D_MODEL // gate_cols
    mixing = ([m_norm] + [functools.partial(m_gate, c) for c in range(n_gate)]
              + [m_fourier, m_attn, m_merge, m_out])
    routing = [r_logits, r_group, r_top1, r_top2, r_rank, r_store]
    for k in range(max(len(mixing), len(routing))):
        if k < len(routing):
            routing[k]()
        if k < len(mixing):
            mixing[k]()


def _merge(x, four, attn, nmix, wg, bg, wf, wa, wo, nffn, wr, cnt_in, tm):
    te = x.shape[0]
    nt = te // tm
    const = lambda i: (0, 0)
    tile = lambda i: (jnp.minimum(i, nt - 1), 0)
    prev = lambda i: (jnp.maximum(i - 1, 0), 0)
    return pl.pallas_call(
        _merge_kernel,
        grid=(nt + 1,),
        in_specs=[
            pl.BlockSpec((tm, D_MODEL), tile),
            pl.BlockSpec((tm, FOURIER_WIDTH), tile),
            pl.BlockSpec((tm, V_WIDTH), tile),
            pl.BlockSpec((1, D_MODEL), const),
            pl.BlockSpec((D_MODEL, 2 * D_MODEL), const),
            pl.BlockSpec((1, 2 * D_MODEL), const),
            pl.BlockSpec((FOURIER_WIDTH, D_MODEL), const),
            pl.BlockSpec((V_WIDTH, D_MODEL), const),
            pl.BlockSpec((D_MODEL, D_MODEL), const),
            pl.BlockSpec((1, D_MODEL), const),
            pl.BlockSpec((2 * ROUTER_LANES, D_MODEL), const),
            pl.BlockSpec((1, ROUTER_LANES), const),
        ],
        out_specs=[
            pl.BlockSpec((tm, D_MODEL), tile),
            pl.BlockSpec((tm, D_MODEL // 2), tile),
            pl.BlockSpec((ROUTER_LANES, tm), lambda i: (0, jnp.maximum(i - 1, 0))),
            pl.BlockSpec((tm, ROUTER_LANES), prev),
            pl.BlockSpec((1, ROUTER_LANES), const),
        ],
        out_shape=[
            jax.ShapeDtypeStruct((te, D_MODEL), F32),
            jax.ShapeDtypeStruct((te, D_MODEL // 2), U32),
            jax.ShapeDtypeStruct((ROUTER_LANES, te), F32),
            jax.ShapeDtypeStruct((te, ROUTER_LANES), F32),
            jax.ShapeDtypeStruct((1, ROUTER_LANES), F32),
        ],
        scratch_shapes=[pltpu.VMEM((1, ROUTER_LANES), F32), pltpu.VMEM((tm, D_MODEL), F32)],
        compiler_params=_cparams(("arbitrary",)),
        name="merge_router",
    )(x, four, attn, nmix, wg, bg, wf, wa, wo, nffn, wr, cnt_in)


def _tile_dest(dest, tm):
    nt = dest.shape[1] // tm
    return dest.reshape(2, nt, tm).transpose(1, 0, 2).reshape(nt, 1, 2 * tm)


def _dispatch_kernel(zs_ref, zl_ref, dest_ref, *rest, tm, tiles):
    n_src = len(tiles)
    src_refs = rest[:n_src]
    xs_ref, zero_sc, sem, zsem = rest[n_src:]
    i = pl.program_id(0)

    def copy_tile(src_ref):
        def row_copy(g, u, slot):
            group = src_ref.at[pl.ds(pl.multiple_of(g * ROW_GROUP, ROW_GROUP), ROW_GROUP)]
            row = dest_ref[0, 0, slot * tm + g * ROW_GROUP + u]
            return pltpu.make_async_copy(group.at[pl.ds(u, 1)], xs_ref.at[pl.ds(row, 1)], sem)

        def start(g, c):
            for u in range(ROW_GROUP):
                row_copy(g, u, 0).start(priority=0)
                row_copy(g, u, 1).start(priority=1)
            return c

        lax.fori_loop(0, tm // ROW_GROUP, start, 0)

        def wait(g, c):
            for u in range(ROW_GROUP):
                row_copy(g, u, 0).wait()
                row_copy(g, u, 1).wait()
            return c

        lax.fori_loop(0, tm // ROW_GROUP, wait, 0)

    first_tile = 0
    for src_ref, n_tiles in zip(src_refs, tiles):
        @pl.when((i >= first_tile) & (i < first_tile + n_tiles))
        def _(src_ref=src_ref):
            copy_tile(src_ref)
        first_tile += n_tiles

    @pl.when(i == 0)
    def _():
        zero_sc[...] = jnp.zeros(zero_sc.shape, U32)

        def zero_copy(r):
            return pltpu.make_async_copy(zero_sc.at[pl.ds(0, 1)], xs_ref.at[pl.ds(r, 1)], zsem)

        def per_expert(e, c):
            def zstart(r, c2):
                zero_copy(zs_ref[e] + r).start()
                return c2
            lax.fori_loop(0, zl_ref[e], zstart, 0)

            def zwait(r, c2):
                zero_copy(zs_ref[e] + r).wait()
                return c2
            lax.fori_loop(0, zl_ref[e], zwait, 0)
            return c
        lax.fori_loop(0, N_EXPERTS, per_expert, 0)

        def block_copy(b):
            return pltpu.make_async_copy(
                zero_sc, xs_ref.at[pl.ds(pl.multiple_of(b * MOE_ROWS, MOE_ROWS), MOE_ROWS)], zsem)

        n_blocks = xs_ref.shape[0] // MOE_ROWS

        def bstart(b, c):
            block_copy(b).start()
            return c
        lax.fori_loop(zl_ref[N_EXPERTS], n_blocks, bstart, 0)

        def bwait(b, c):
            block_copy(b).wait()
            return c
        lax.fori_loop(zl_ref[N_EXPERTS], n_blocks, bwait, 0)


def _dispatch(zstart, zlen, dests, srcs, n_rows, tm):
    tiles = tuple(s.shape[0] // tm for s in srcs)
    dest3 = jnp.concatenate([_tile_dest(d, tm) for d in dests], axis=0)
    src_specs = []
    first_tile = 0
    for n in tiles:
        src_specs.append(pl.BlockSpec(
            (tm, D_MODEL // 2), lambda i, zs, zl, ft=first_tile, n=n: (jnp.clip(i - ft, 0, n - 1), 0)))
        first_tile += n
    grid_spec = pltpu.PrefetchScalarGridSpec(
        num_scalar_prefetch=2,
        grid=(sum(tiles),),
        in_specs=[pl.BlockSpec((1, 1, 2 * tm), lambda i, zs, zl: (i, 0, 0), memory_space=pltpu.SMEM)]
        + src_specs,
        out_specs=pl.BlockSpec(memory_space=pl.ANY),
        scratch_shapes=[pltpu.VMEM((MOE_ROWS, D_MODEL // 2), U32), pltpu.SemaphoreType.DMA,
                        pltpu.SemaphoreType.DMA],
    )
    return pl.pallas_call(
        functools.partial(_dispatch_kernel, tm=tm, tiles=tiles),
        grid_spec=grid_spec,
        out_shape=jax.ShapeDtypeStruct((n_rows, D_MODEL // 2), U32),
        compiler_params=_cparams(("arbitrary",)),
        name="moe_dispatch",
    )(zstart, zlen, dest3, *srcs)


def _expert_kernel(be_ref, nu_ref, ord_ref, nxt_ref, xs_ref, wg_hbm, wu_hbm, wd_hbm, o_ref,
                   wbuf, wgb, wub, wdb, sem):
    b = pl.program_id(0)
    used = b < nu_ref[0]
    e = be_ref[b]
    new_expert = (b == 0) | (e != be_ref[jnp.maximum(b - 1, 0)])
    slot = ord_ref[b] & 1

    def fetch(expert, s):
        return [pltpu.make_async_copy(w.at[expert], wbuf.at[s, i], sem.at[s])
                for i, w in enumerate((wg_hbm, wu_hbm, wd_hbm))]

    @pl.when((b == 0) & used)
    def _():
        for cp in fetch(e, 0):
            cp.start()

    @pl.when(used & new_expert)
    def _():
        for cp in fetch(e, slot):
            cp.wait()

        @pl.when(nxt_ref[b] >= 0)
        def _():
            for cp in fetch(nxt_ref[b], 1 - slot):
                cp.start()

        rows = 128

        def cast(r, c):
            r0 = pl.multiple_of(r * rows, rows)
            wgb[pl.ds(r0, rows), :] = wbuf[slot, 0, pl.ds(r0, rows), :].astype(BF16)
            wub[pl.ds(r0, rows), :] = wbuf[slot, 1, pl.ds(r0, rows), :].astype(BF16)
            wdb[pl.ds(r0, rows), :] = wbuf[slot, 2, pl.ds(r0, rows), :].astype(BF16)
            return c
        lax.fori_loop(0, D_MODEL // rows, cast, 0)

    @pl.when(used)
    def _():
        lo, hi = _unpack_bf16_pairs(xs_ref[...])
        x = jnp.concatenate([lo, hi], axis=1).astype(BF16)
        g = jnp.dot(x, wgb[...], preferred_element_type=F32)
        u = jnp.dot(x, wub[...], preferred_element_type=F32)
        hmid = (g * jax.nn.sigmoid(g) * u).astype(BF16)
        y = jnp.dot(hmid, wdb[...], preferred_element_type=F32)
        o_ref[...] = _pack_bf16_pairs(y[:, :D_MODEL // 2], y[:, D_MODEL // 2:])

    @pl.when(jnp.logical_not(used))
    def _():
        o_ref[...] = jnp.zeros(o_ref.shape, U32)


def _experts(block_expert, n_used, block_ord, block_next, xs, wg, wu, wd):
    assert EXPERT_FF == D_MODEL
    n_blocks = xs.shape[0] // MOE_ROWS
    row_spec = pl.BlockSpec((MOE_ROWS, D_MODEL // 2), lambda b, *_: (b, 0))
    grid_spec = pltpu.PrefetchScalarGridSpec(
        num_scalar_prefetch=4,
        grid=(n_blocks,),
        in_specs=[row_spec] + [pl.BlockSpec(memory_space=pl.ANY)] * 3,
        out_specs=row_spec,
        scratch_shapes=[pltpu.VMEM((2, 3, D_MODEL, EXPERT_FF), F32),
                        pltpu.VMEM((D_MODEL, EXPERT_FF), BF16), pltpu.VMEM((D_MODEL, EXPERT_FF), BF16),
                        pltpu.VMEM((EXPERT_FF, D_MODEL), BF16), pltpu.SemaphoreType.DMA((2,))],
    )
    return pl.pallas_call(
        _expert_kernel,
        grid_spec=grid_spec,
        out_shape=jax.ShapeDtypeStruct((xs.shape[0], D_MODEL // 2), U32),
        compiler_params=_cparams(("arbitrary",)),
        name="moe_experts",
    )(block_expert, n_used, block_ord, block_next, xs, wg, wu, wd)


def _combine_kernel(dest_ref, ys_ref, h_ref, rw_ref, nw_ref, o_ref, buf, sem, *, tm):
    def row_copy(g, u, slot):
        group = buf.at[slot, pl.ds(pl.multiple_of(g * ROW_GROUP, ROW_GROUP), ROW_GROUP)]
        row = dest_ref[0, 0, slot * tm + g * ROW_GROUP + u]
        return pltpu.make_async_copy(ys_ref.at[pl.ds(row, 1)], group.at[pl.ds(u, 1)], sem)

    def start(g, c):
        for u in range(ROW_GROUP):
            row_copy(g, u, 0).start(priority=0)
            row_copy(g, u, 1).start(priority=1)
        return c

    lax.fori_loop(0, tm // ROW_GROUP, start, 0)

    def wait(g, c):
        for u in range(ROW_GROUP):
            row_copy(g, u, 0).wait()
            row_copy(g, u, 1).wait()
        return c

    lax.fori_loop(0, tm // ROW_GROUP, wait, 0)

    rw = rw_ref[...]
    w1 = rw[:, 0:1]
    w2 = rw[:, 1:2]
    lo1, hi1 = _unpack_bf16_pairs(buf[0])
    lo2, hi2 = _unpack_bf16_pairs(buf[1])
    y = jnp.concatenate([lo1 * w1 + lo2 * w2, hi1 * w1 + hi2 * w2], axis=1)
    h = h_ref[...] + y
    o_ref[...] = h * lax.rsqrt(jnp.mean(h * h, axis=-1, keepdims=True) + NORM_EPS) * nw_ref[...]


def _combine(dest, ys, h, rw, nw, tm):
    te = h.shape[0]
    nt = te // tm
    dest3 = _tile_dest(dest, tm)
    return pl.pallas_call(
        functools.partial(_combine_kernel, tm=tm),
        grid=(nt,),
        in_specs=[
            pl.BlockSpec((1, 1, 2 * tm), lambda i: (i, 0, 0), memory_space=pltpu.SMEM),
            pl.BlockSpec(memory_space=pl.ANY),
            pl.BlockSpec((tm, D_MODEL), lambda i: (i, 0)),
            pl.BlockSpec((tm, ROUTER_LANES), lambda i: (i, 0)),
            pl.BlockSpec((1, D_MODEL), lambda i: (0, 0)),
        ],
        out_specs=pl.BlockSpec((tm, D_MODEL), lambda i: (i, 0)),
        out_shape=jax.ShapeDtypeStruct((te, D_MODEL), F32),
        scratch_shapes=[pltpu.VMEM((2, tm, D_MODEL // 2), U32), pltpu.SemaphoreType.DMA],
        compiler_params=_cparams(("arbitrary",)),
        name="moe_combine",
    )(dest3, ys, h, rw, nw)


def _split_router_weight(w_group, w_expert):
    wrt = jnp.concatenate([w_group.T, w_expert.T], axis=0)
    wrt = jnp.pad(wrt, ((0, ROUTER_LANES - wrt.shape[0]), (0, 0)))
    hi = wrt.astype(BF16)
    lo = (wrt - hi.astype(F32)).astype(BF16)
    return jnp.concatenate([hi, lo], axis=0)


def _rope_tables(seq):
    pos = jnp.arange(seq, dtype=F32)
    inv_freq = ROPE_THETA ** (-jnp.arange(0, HEAD_DIM, 2, dtype=F32) / HEAD_DIM)
    ang = pos[:, None] * inv_freq[None, :]
    c, s = jnp.cos(ang), jnp.sin(ang)
    cos = jnp.concatenate([c, c, c, c], axis=1)
    sin = jnp.concatenate([-s, s, -s, s], axis=1)
    return cos, sin


def _token_mixer(x2d, batch, seq, p, cnt_in):
    cos, sin = _rope_tables(seq)
    f, qz, k, vt = _inproj(x2d, p["nmix"], p["w1"], cos, sin, tm=512)
    four = _fourier(f, batch, seq)
    attn = _attention(qz, k, vt, p["lam"], p["sub"], batch, seq, tq=512, ck=min(2048, seq // 2),
                      sb=256, kv_block=seq)
    return _merge(x2d, four, attn, p["nmix"], p["wgate"], p["bg"], p["wf"], p["wa"], p["wo"],
                  p["nffn"], p["wr"], cnt_in, tm=512)


def kernel(x_prompt, x_sample, norm_mix_w, w_in, b_gate, w_fourier, lambda_q1, lambda_k1, lambda_q2,
           lambda_k2, subln_w, w_attn, w_out, norm_ffn_w, w_group_router, w_expert_router,
           w_expert_gate, w_expert_up, w_expert_down, norm_final_w):
    wr = _split_router_weight(w_group_router[0], w_expert_router[0])
    p = {
        "nmix": norm_mix_w[0][None, :],
        "w1": w_in[0][:, :2048].astype(BF16),
        "wgate": w_in[0][:, 2048:].astype(BF16),
        "bg": b_gate[0][None, :],
        "wf": w_fourier[0].astype(BF16),
        "wa": w_attn[0].astype(BF16),
        "wo": w_out[0].astype(BF16),
        "nffn": norm_ffn_w[0][None, :],
        "wr": wr,
        "lam": jnp.stack([lambda_q1[0], lambda_k1[0], lambda_q2[0], lambda_k2[0]]),
        "sub": subln_w[0][:, None],
    }
    inputs = [x_prompt, x_sample]
    mixed = []
    cnt = jnp.zeros((1, ROUTER_LANES), F32)
    for x in inputs:
        batch, seq, _ = x.shape
        h, hnp, ri, rw, cnt = _token_mixer(x.reshape(batch * seq, D_MODEL), batch, seq, p, cnt)
        mixed.append((h, hnp, ri, rw))
    outs = _moe_and_final(mixed, cnt, w_expert_gate[0], w_expert_up[0], w_expert_down[0], norm_final_w)
    return tuple(o.reshape(x.shape) for o, x in zip(outs, inputs))


def _moe_and_final(mixed, cnt, w_gate, w_up, w_down, norm_final_w):
    n_assign = 2 * sum(m[0].shape[0] for m in mixed)
    n_blocks = -(-(n_assign + N_EXPERTS * (MOE_ROWS - 1)) // MOE_ROWS)
    n_rows = n_blocks * MOE_ROWS
    counts = cnt[0, EXPERT_LANE0:EXPERT_LANE0 + N_EXPERTS].astype(I32)
    padded = (counts + MOE_ROWS - 1) // MOE_ROWS * MOE_ROWS
    pend = jnp.cumsum(padded)
    pstart = pend - padded
    total = pend[-1]
    bstart = jnp.arange(n_blocks, dtype=I32) * MOE_ROWS
    be = jnp.minimum(jnp.sum((pend[None, :] <= bstart[:, None]).astype(I32), axis=1), N_EXPERTS - 1)
    n_used = (total // MOE_ROWS).astype(I32)
    be = jnp.where(bstart < total, be, be[jnp.maximum(n_used - 1, 0)])
    block_ord = jnp.cumsum(jnp.concatenate([jnp.zeros((1,), I32), (be[1:] != be[:-1]).astype(I32)]))
    eidx = jnp.arange(N_EXPERTS, dtype=I32)
    later_used = (eidx[None, :] > eidx[:, None]) & (padded[None, :] > 0)
    next_expert = jnp.min(jnp.where(later_used, eidx[None, :], N_EXPERTS), axis=1)
    next_expert = jnp.where(next_expert == N_EXPERTS, -1, next_expert)
    block_next = next_expert[be]
    zstart = pstart + counts
    zlen = jnp.concatenate([(pend - zstart).astype(I32), n_used[None]])

    def row_of(expert, rank):
        hit = expert[None].astype(I32) == eidx[:, None, None]
        return jnp.sum(jnp.where(hit, pstart[:, None, None], 0), axis=0) + rank.astype(I32)

    dests = [row_of(m[2][0:2], m[2][2:4]) for m in mixed]
    xs = _dispatch(zstart, zlen, dests, [m[1] for m in mixed], n_rows, tm=1024)
    ys = _experts(be, n_used[None], block_ord, block_next, xs, w_gate, w_up, w_down)
    return [_combine(dest, ys, m[0], m[3], norm_final_w[None, :], tm=512)
            for m, dest in zip(mixed, dests)]
```

```python
import functools
import math

import numpy as np
import jax
import jax.numpy as jnp
from jax import lax
from jax.experimental import pallas as pl
from jax.experimental.pallas import tpu as pltpu

F32 = jnp.float32
BF16 = jnp.bfloat16
U32 = jnp.uint32
I32 = jnp.int32

D_MODEL = 1024
FOURIER_WIDTH = 512
FOURIER_GROUPS = 4
GROUP_DIM = 128
HEADS = 4
HEAD_DIM = 64
QK_WIDTH = 512
V_DIM = 128
V_WIDTH = 512
ROPE_THETA = 10000.0
N_GROUPS = 4
EXPERTS_PER_GROUP = 8
N_EXPERTS = 32
EXPERT_FF = 1024
NORM_EPS = 1e-6
LAMBDA_INIT = 0.8 - 0.6 * math.exp(0.0)

LANES = 128
ROUTER_LANES = 128
EXPERT_LANE0 = N_GROUPS
MOE_ROWS = 256
ROW_GROUP = 8
HEAD_SHIFT = HEADS.bit_length() - 1
MIX_WIDTH = FOURIER_WIDTH + 2 * QK_WIDTH + V_WIDTH

INPROJ_TOKENS = 512
ATTN_QUERIES = 512
ATTN_KEYS_MAX = 4096
ATTN_SUB_KEYS = 256
MERGE_TOKENS = 512
DISPATCH_TOKENS = 1024
COMBINE_TOKENS = 512
DFT_N = 128
VMEM_LIMIT = 56 * 1024 * 1024

LOG2E = 1.4426950408889634
NEG_BIG = -3.0e38


def _cparams(sem, vmem=VMEM_LIMIT, flags=None):
    return pltpu.CompilerParams(dimension_semantics=sem, vmem_limit_bytes=vmem, flags=flags)


def _swap_halves(t):
    n = t.shape[1]
    lane = lax.broadcasted_iota(I32, t.shape, 1)
    first = (lane % HEAD_DIM) < (HEAD_DIM // 2)
    return jnp.where(first, pltpu.roll(t, n - HEAD_DIM // 2, 1), pltpu.roll(t, HEAD_DIM // 2, 1))


def _inproj_kernel(x_ref, nw_ref, w_ref, cos_ref, sin_ref, f_ref, qz_ref, k_ref, vt_ref):
    x = x_ref[...]
    xn = x * lax.rsqrt(jnp.mean(x * x, axis=-1, keepdims=True) + NORM_EPS) * nw_ref[...]
    proj = jnp.dot(xn.astype(BF16), w_ref[...], preferred_element_type=F32)
    f_ref[...] = proj[:, :FOURIER_WIDTH].astype(BF16)
    q = proj[:, FOURIER_WIDTH:FOURIER_WIDTH + QK_WIDTH]
    k = proj[:, FOURIER_WIDTH + QK_WIDTH:FOURIER_WIDTH + 2 * QK_WIDTH]
    v = proj[:, FOURIER_WIDTH + 2 * QK_WIDTH:]
    cos = jnp.concatenate([cos_ref[...]] * (QK_WIDTH // LANES), axis=1)
    sin = jnp.concatenate([sin_ref[...]] * (QK_WIDTH // LANES), axis=1)
    q = q * cos + _swap_halves(q) * sin
    k = k * cos + _swap_halves(k) * sin
    q = q * (HEAD_DIM ** -0.5 * LOG2E)
    lane = lax.broadcasted_iota(I32, (q.shape[0], LANES), 1)
    lo = lane < HEAD_DIM
    vt = v.T.astype(BF16)
    for h in range(HEADS):
        qh = q[:, h * LANES:(h + 1) * LANES]
        qz_ref[2 * h] = jnp.where(lo, qh, 0.0).astype(BF16)
        qz_ref[2 * h + 1] = jnp.where(lo, 0.0, qh).astype(BF16)
        k_ref[h] = k[:, h * LANES:(h + 1) * LANES].astype(BF16)
        vt_ref[0, h] = vt[h * V_DIM:(h + 1) * V_DIM, :]


def _inproj(x, nw, w1, cos, sin, tm):
    te = x.shape[0]
    nt = te // tm
    pos_tiles = cos.shape[0] // tm
    return pl.pallas_call(
        _inproj_kernel,
        grid=(nt,),
        in_specs=[
            pl.BlockSpec((tm, D_MODEL), lambda i: (i, 0)),
            pl.BlockSpec((1, D_MODEL), lambda i: (0, 0)),
            pl.BlockSpec((D_MODEL, MIX_WIDTH), lambda i: (0, 0)),
            pl.BlockSpec((tm, LANES), lambda i: (i % pos_tiles, 0)),
            pl.BlockSpec((tm, LANES), lambda i: (i % pos_tiles, 0)),
        ],
        out_specs=[
            pl.BlockSpec((tm, FOURIER_WIDTH), lambda i: (i, 0)),
            pl.BlockSpec((2 * HEADS, tm, LANES), lambda i: (0, i, 0)),
            pl.BlockSpec((HEADS, tm, LANES), lambda i: (0, i, 0)),
            pl.BlockSpec((1, HEADS, V_DIM, tm), lambda i: (i, 0, 0, 0)),
        ],
        out_shape=[
            jax.ShapeDtypeStruct((te, FOURIER_WIDTH), BF16),
            jax.ShapeDtypeStruct((2 * HEADS, te, LANES), BF16),
            jax.ShapeDtypeStruct((HEADS, te, LANES), BF16),
            jax.ShapeDtypeStruct((nt, HEADS, V_DIM, tm), BF16),
        ],
        compiler_params=_cparams(("parallel",)),
        name="inproj",
    )(x, nw, w1, cos, sin)


@functools.lru_cache(maxsize=None)
def _dft_tables(batch, seq):
    n1 = seq // DFT_N
    assert batch * n1 == DFT_N
    idx = np.arange(n1)
    ang = 2.0 * np.pi * ((idx[:, None] * idx[None, :]) % n1) / n1
    eye = np.eye(batch)
    m1 = np.concatenate([np.kron(eye, np.cos(ang)), -np.kron(eye, np.sin(ang))], axis=0)
    k1 = np.arange(n1)[:, None, None]
    k2 = np.arange(DFT_N)[None, :, None]
    n2 = np.arange(DFT_N)[None, None, :]
    ang2 = 2.0 * np.pi * ((n2 * (k1 + n1 * k2)) % seq) / seq
    er, ei = np.cos(ang2), -np.sin(ang2)
    m2 = np.concatenate([np.concatenate([er, -ei], axis=2), np.concatenate([ei, er], axis=2)], axis=1)
    c = np.arange(GROUP_DIM)
    ang3 = 2.0 * np.pi * ((c[:, None] * c[None, :]) % GROUP_DIM) / GROUP_DIM
    norm = 1.0 / math.sqrt(seq * GROUP_DIM)
    cs = np.concatenate([np.cos(ang3), np.sin(ang3)], axis=0) * norm
    return (jnp.asarray(m1, BF16), jnp.asarray(m2, BF16), jnp.asarray(cs, BF16))


def _dft1_kernel(m_ref, x_ref, y_ref):
    y_ref[...] = jnp.dot(m_ref[...], x_ref[...], preferred_element_type=F32).astype(BF16)


def _dft1(m1, x2d, tn=4096):
    n = x2d.shape[1]
    return pl.pallas_call(
        _dft1_kernel,
        grid=(n // tn,),
        in_specs=[pl.BlockSpec((2 * DFT_N, DFT_N), lambda i: (0, 0)),
                  pl.BlockSpec((DFT_N, tn), lambda i: (0, i))],
        out_specs=pl.BlockSpec((2 * DFT_N, tn), lambda i: (0, i)),
        out_shape=jax.ShapeDtypeStruct((2 * DFT_N, n), BF16),
        compiler_params=_cparams(("parallel",)),
        name="dft_stage1",
    )(m1, x2d)


def _dft2_kernel(m_ref, yr_ref, yi_ref, cs_ref, o_ref, *, rows_per_step):
    for j in range(rows_per_step):
        m = m_ref[j]
        z = (jnp.dot(m[:, :DFT_N], yr_ref[0, j], preferred_element_type=F32)
             + jnp.dot(m[:, DFT_N:], yi_ref[0, j], preferred_element_type=F32))
        zb = z.astype(BF16)
        for g in range(FOURIER_GROUPS):
            zg = jnp.concatenate([zb[:DFT_N, g * LANES:(g + 1) * LANES],
                                  zb[DFT_N:, g * LANES:(g + 1) * LANES]], axis=1)
            o_ref[0, :, (j * FOURIER_GROUPS + g) * LANES:(j * FOURIER_GROUPS + g + 1) * LANES] = (
                jnp.dot(zg, cs_ref[...], preferred_element_type=F32).astype(BF16))


def _dft2(m2, y, cs, batch, seq, rows_per_step=4):
    n1 = seq // DFT_N
    r = rows_per_step
    steps = DFT_N // r
    per_b = n1 // r
    return pl.pallas_call(
        functools.partial(_dft2_kernel, rows_per_step=r),
        grid=(steps,),
        in_specs=[
            pl.BlockSpec((r, 2 * DFT_N, 2 * DFT_N), lambda i: (i % per_b, 0, 0)),
            pl.BlockSpec((1, r, DFT_N, FOURIER_WIDTH), lambda i: (0, i, 0, 0)),
            pl.BlockSpec((1, r, DFT_N, FOURIER_WIDTH), lambda i: (1, i, 0, 0)),
            pl.BlockSpec((2 * GROUP_DIM, GROUP_DIM), lambda i: (0, 0)),
        ],
        out_specs=pl.BlockSpec((1, DFT_N, r * FOURIER_WIDTH), lambda i: (i // per_b, 0, i % per_b)),
        out_shape=jax.ShapeDtypeStruct((batch, DFT_N, n1 * FOURIER_WIDTH), BF16),
        compiler_params=_cparams(("parallel",)),
        name="dft_stage2",
    )(m2, y, y, cs)


def _fourier(f, batch, seq):
    m1, m2, cs = _dft_tables(batch, seq)
    y = _dft1(m1, f.reshape(DFT_N, DFT_N * FOURIER_WIDTH))
    out = _dft2(m2, y.reshape(2, DFT_N, DFT_N, FOURIER_WIDTH), cs, batch, seq)
    return out.reshape(batch * seq, FOURIER_WIDTH)


def _attn_kernel(qz_ref, k_ref, vt_ref, lam_ref, sub_ref, o_ref,
                 s_a, s_b, mx_a, mx_b, m_a, m_b, l_a, l_b, acc_a, acc_b, *, ck, sb, n_chunks):
    j = pl.program_id(2)
    n_items = n_chunks * HEADS

    @pl.when(j == 0)
    def _():
        for m_sc, l_sc, acc_sc in ((m_a, l_a, acc_a), (m_b, l_b, acc_b)):
            m_sc[...] = jnp.full(m_sc.shape, NEG_BIG, F32)
            l_sc[...] = jnp.zeros(l_sc.shape, F32)
            acc_sc[...] = jnp.zeros(acc_sc.shape, F32)

    def split(t):
        t = jnp.asarray(t, I32)
        return lax.shift_right_logical(t, HEAD_SHIFT), t & (HEADS - 1)

    ckv = vt_ref.shape[-1]
    n_sub = ck // sb
    bufs = ((s_a, mx_a, m_a, l_a, acc_a), (s_b, mx_b, m_b, l_b, acc_b))

    def step(t, do_update, do_scores):
        if do_update:
            c, h = split(t)
        if do_scores:
            c1, h1 = split(t + 1)
        state = []
        for cc, (s_ref, mx_ref, m_sc, l_sc, acc_sc) in enumerate(bufs):
            if do_update:
                m_old = m_sc[h]
                m_new = jnp.maximum(m_old, mx_ref[...])
                state.append([m_old, m_new, None, None, None])
            else:
                state.append([None, None, None, None, None])
        for u in range(n_sub):
            rows = pl.ds(u * sb, sb)
            for cc, (s_ref, mx_ref, m_sc, l_sc, acc_sc) in enumerate(bufs):
                st = state[cc]
                if do_update:
                    p = jnp.exp2(s_ref[rows, :] - st[1])
                    ps = jnp.sum(p, axis=0, keepdims=True)
                    lane0 = (u * sb) % ckv
                    vth = vt_ref[c * (ck // ckv) + (u * sb) // ckv, h, :, lane0:lane0 + sb]
                    d = jnp.dot(vth, p.astype(BF16), preferred_element_type=F32)
                    st[2] = ps if st[2] is None else st[2] + ps
                    st[3] = d if st[3] is None else st[3] + d
                if do_scores:
                    row0 = pl.multiple_of(c1 * ck + u * sb, sb)
                    s = lax.dot_general(k_ref[h1, pl.ds(row0, sb), :], qz_ref[2 * h1 + cc],
                                        (((1,), (1,)), ((), ())), preferred_element_type=F32)
                    s_ref[rows, :] = s
                    mx = jnp.max(s, axis=0, keepdims=True)
                    st[4] = mx if st[4] is None else jnp.maximum(st[4], mx)
        for cc, (s_ref, mx_ref, m_sc, l_sc, acc_sc) in enumerate(bufs):
            m_old, m_new, l_add, pv, mx = state[cc]
            if do_update:
                alpha = jnp.exp2(m_old - m_new)
                l_sc[h] = alpha * l_sc[h] + l_add
                acc_sc[h] = alpha * acc_sc[h] + pv
                m_sc[h] = m_new
            if do_scores:
                mx_ref[...] = mx

    step(-1, False, True)

    def body(t, carry):
        step(t, True, True)
        return carry

    lax.fori_loop(0, n_items - 1, body, 0)
    step(n_items - 1, True, False)

    @pl.when(j == pl.num_programs(2) - 1)
    def _():
        lv = lam_ref[...]
        lam = (jnp.exp(jnp.sum(lv[0:1] * lv[1:2], axis=1, keepdims=True))
               - jnp.exp(jnp.sum(lv[2:3] * lv[3:4], axis=1, keepdims=True)) + LAMBDA_INIT)
        for h in range(HEADS):
            o = acc_a[h] / l_a[h] - lam * (acc_b[h] / l_b[h])
            o = o * lax.rsqrt(jnp.mean(o * o, axis=0, keepdims=True) + NORM_EPS)
            o = o * sub_ref[...] * (1.0 - LAMBDA_INIT)
            o_ref[:, h * V_DIM:(h + 1) * V_DIM] = o.T.astype(BF16)


def _attention(qz, k, vt, lam_vecs, sub_col, batch, seq, tq, ck, sb, kv_block):
    nq = seq // tq
    nkb = seq // kv_block
    n_chunks = kv_block // ck
    ckv = vt.shape[-1]
    stat = pltpu.VMEM((HEADS, 1, tq), F32)
    kv_mode = dict(pipeline_mode=pl.Buffered(1)) if batch * nkb == 1 else {}
    return pl.pallas_call(
        functools.partial(_attn_kernel, ck=ck, sb=sb, n_chunks=n_chunks),
        grid=(batch, nq, nkb),
        in_specs=[
            pl.BlockSpec((2 * HEADS, tq, LANES), lambda b, i, j: (0, b * nq + i, 0)),
            pl.BlockSpec((HEADS, kv_block, LANES), lambda b, i, j: (0, b * nkb + j, 0), **kv_mode),
            pl.BlockSpec((kv_block // ckv, HEADS, V_DIM, ckv), lambda b, i, j: (b * nkb + j, 0, 0, 0),
                         **kv_mode),
            pl.BlockSpec((4, HEAD_DIM), lambda b, i, j: (0, 0)),
            pl.BlockSpec((V_DIM, 1), lambda b, i, j: (0, 0)),
        ],
        out_specs=pl.BlockSpec((tq, V_WIDTH), lambda b, i, j: (b * nq + i, 0)),
        out_shape=jax.ShapeDtypeStruct((batch * seq, V_WIDTH), BF16),
        scratch_shapes=[
            pltpu.VMEM((ck, tq), F32), pltpu.VMEM((ck, tq), F32),
            pltpu.VMEM((1, tq), F32), pltpu.VMEM((1, tq), F32),
            stat, stat, stat, stat,
            pltpu.VMEM((HEADS, V_DIM, tq), F32), pltpu.VMEM((HEADS, V_DIM, tq), F32),
        ],
        compiler_params=_cparams(("parallel", "parallel", "arbitrary")),
        name="diff_attention",
    )(qz, k, vt, lam_vecs, sub_col)


def _pack_bf16_pairs(a, b):
    ab = lax.bitcast_convert_type(a.astype(BF16).astype(F32), U32)
    bb = lax.bitcast_convert_type(b.astype(BF16).astype(F32), U32)
    return (bb & jnp.uint32(0xFFFF0000)) | (ab >> 16)


def _unpack_bf16_pairs(w):
    lo = lax.bitcast_convert_type(w << 16, F32)
    hi = lax.bitcast_convert_type(w & jnp.uint32(0xFFFF0000), F32)
    return lo, hi


def _merge_kernel(x_ref, four_ref, attn_ref, nmix_ref, wg_ref, bg_ref, wf_ref, wa_ref, wo_ref,
                  nffn_ref, wr_ref, cnt_in_ref,
                  h_ref, hnp_ref, ri_ref, rw_ref, cnt_ref, carry_sc, hn_sc):
    i = pl.program_id(0)

    @pl.when(i == 0)
    def _():
        carry_sc[...] = cnt_in_ref[...]
        hn_sc[...] = jnp.zeros(hn_sc.shape, F32)

    st = {}
    tm = hn_sc.shape[0]
    lane = lax.broadcasted_iota(I32, (tm, ROUTER_LANES), 1)

    def r_logits():
        hn = hn_sc[...]
        hn_hi = hn.astype(BF16)
        hn_lo = (hn - hn_hi.astype(F32)).astype(BF16)
        parts = lax.dot_general(jnp.concatenate([hn_hi, hn_lo], axis=0), wr_ref[...],
                                (((1,), (1,)), ((), ())), preferred_element_type=F32)
        st["lg"] = (((parts[tm:, ROUTER_LANES:] + parts[:tm, ROUTER_LANES:])
                     + parts[tm:, :ROUTER_LANES]) + parts[:tm, :ROUTER_LANES])

    def r_group():
        gmask = lane < N_GROUPS
        gl = jnp.where(gmask, st["lg"], NEG_BIG)
        gmax = jnp.max(gl, axis=1, keepdims=True)
        st["g_idx"] = jnp.min(jnp.where(gl == gmax, lane, ROUTER_LANES), axis=1, keepdims=True)
        st["g_w"] = 1.0 / jnp.sum(jnp.where(gmask, jnp.exp(gl - gmax), 0.0), axis=1, keepdims=True)

    def r_top1():
        e_lane = lane - EXPERT_LANE0
        emask = ((e_lane >= 0) & (e_lane < N_EXPERTS)
                 & ((e_lane // EXPERTS_PER_GROUP) == st["g_idx"]))
        el = jnp.where(emask, st["lg"], NEG_BIG)
        st["v1"] = jnp.max(el, axis=1, keepdims=True)
        st["i1"] = jnp.min(jnp.where(el == st["v1"], lane, ROUTER_LANES), axis=1, keepdims=True)
        st["el"] = el

    def r_top2():
        el2 = jnp.where(lane == st["i1"], NEG_BIG, st["el"])
        v2 = jnp.max(el2, axis=1, keepdims=True)
        st["i2"] = jnp.min(jnp.where(el2 == v2, lane, ROUTER_LANES), axis=1, keepdims=True)
        s21 = jnp.exp(v2 - st["v1"])
        w1 = st["g_w"] / (1.0 + s21)
        w2 = st["g_w"] * s21 / (1.0 + s21)
        rw_ref[...] = jnp.where(lane == 0, w1, jnp.where(lane == 1, w2, 0.0))

    def r_rank():
        oh1 = lane == st["i1"]
        oh2 = lane == st["i2"]
        ohs = (oh1 | oh2).astype(BF16)
        row = lax.broadcasted_iota(I32, (tm, tm), 0)
        col = lax.broadcasted_iota(I32, (tm, tm), 1)
        tril = (col < row).astype(BF16)
        before = jnp.dot(tril, ohs, preferred_element_type=F32) + carry_sc[...]
        st["r1"] = jnp.sum(jnp.where(oh1, before, 0.0), axis=1, keepdims=True)
        st["r2"] = jnp.sum(jnp.where(oh2, before, 0.0), axis=1, keepdims=True)
        real_tile = (i > 0).astype(F32)
        carry_sc[...] = carry_sc[...] + real_tile * jnp.sum(ohs.astype(F32), axis=0, keepdims=True)
        cnt_ref[...] = carry_sc[...]

    def r_store():
        e1 = (st["i1"] - EXPERT_LANE0).astype(F32)
        e2 = (st["i2"] - EXPERT_LANE0).astype(F32)
        ri = jnp.where(lane == 0, e1, jnp.where(lane == 1, e2,
             jnp.where(lane == 2, st["r1"], jnp.where(lane == 3, st["r2"], 0.0))))
        ri_ref[...] = ri.T

    gate_cols = 512

    def m_norm():
        x = x_ref[...]
        xn = x * lax.rsqrt(jnp.mean(x * x, axis=-1, keepdims=True) + NORM_EPS) * nmix_ref[...]
        st["xnb"] = xn.astype(BF16)
        st["gates"] = []

    def m_gate(c):
        cols = slice(c * gate_cols, (c + 1) * gate_cols)
        st["gates"].append(jax.nn.sigmoid(
            jnp.dot(st["xnb"], wg_ref[:, cols], preferred_element_type=F32) + bg_ref[:, cols]))

    def m_fourier():
        st["bf"] = jnp.dot(four_ref[...], wf_ref[...], preferred_element_type=F32)

    def m_attn():
        st["ba"] = jnp.dot(attn_ref[...], wa_ref[...], preferred_element_type=F32)

    def m_merge():
        gates = jnp.concatenate(st["gates"], axis=1)
        st["merged"] = (gates[:, :D_MODEL] * st["bf"] + gates[:, D_MODEL:] * st["ba"]).astype(BF16)

    def m_out():
        h = x_ref[...] + jnp.dot(st["merged"], wo_ref[...], preferred_element_type=F32)
        h_ref[...] = h
        hn_next = h * lax.rsqrt(jnp.mean(h * h, axis=-1, keepdims=True) + NORM_EPS) * nffn_ref[...]
        hnp_ref[...] = _pack_bf16_pairs(hn_next[:, :D_MODEL // 2], hn_next[:, D_MODEL // 2:])
        hn_sc[...] = hn_next

    n_gate = 2 * D_MODEL // gate_cols
    mixing = ([m_norm] + [functools.partial(m_gate, c) for c in range(n_gate)]
              + [m_fourier, m_attn, m_merge, m_out])
    routing = [r_logits, r_group, r_top1, r_top2, r_rank, r_store]
    for k in range(max(len(mixing), len(routing))):
        if k < len(routing):
            routing[k]()
        if k < len(mixing):
            mixing[k]()


def _merge(x, four, attn, nmix, wg, bg, wf, wa, wo, nffn, wr, cnt_in, tm):
    te = x.shape[0]
    nt = te // tm
    const = lambda i: (0, 0)
    tile = lambda i: (jnp.minimum(i, nt - 1), 0)
    prev = lambda i: (jnp.maximum(i - 1, 0), 0)
    return pl.pallas_call(
        _merge_kernel,
        grid=(nt + 1,),
        in_specs=[
            pl.BlockSpec((tm, D_MODEL), tile),
            pl.BlockSpec((tm, FOURIER_WIDTH), tile),
            pl.BlockSpec((tm, V_WIDTH), tile),
            pl.BlockSpec((1, D_MODEL), const),
            pl.BlockSpec((D_MODEL, 2 * D_MODEL), const),
            pl.BlockSpec((1, 2 * D_MODEL), const),
            pl.BlockSpec((FOURIER_WIDTH, D_MODEL), const),
            pl.BlockSpec((V_WIDTH, D_MODEL), const),
            pl.BlockSpec((D_MODEL, D_MODEL), const),
            pl.BlockSpec((1, D_MODEL), const),
            pl.BlockSpec((2 * ROUTER_LANES, D_MODEL), const),
            pl.BlockSpec((1, ROUTER_LANES), const),
        ],
        out_specs=[
            pl.BlockSpec((tm, D_MODEL), tile),
            pl.BlockSpec((tm, D_MODEL // 2), tile),
            pl.BlockSpec((ROUTER_LANES, tm), lambda i: (0, jnp.maximum(i - 1, 0))),
            pl.BlockSpec((tm, ROUTER_LANES), prev),
            pl.BlockSpec((1, ROUTER_LANES), const),
        ],
        out_shape=[
            jax.ShapeDtypeStruct((te, D_MODEL), F32),
            jax.ShapeDtypeStruct((te, D_MODEL // 2), U32),
            jax.ShapeDtypeStruct((ROUTER_LANES, te), F32),
            jax.ShapeDtypeStruct((te, ROUTER_LANES), F32),
            jax.ShapeDtypeStruct((1, ROUTER_LANES), F32),
        ],
        scratch_shapes=[pltpu.VMEM((1, ROUTER_LANES), F32), pltpu.VMEM((tm, D_MODEL), F32)],
        compiler_params=_cparams(("arbitrary",)),
        name="merge_router",
    )(x, four, attn, nmix, wg, bg, wf, wa, wo, nffn, wr, cnt_in)


def _tile_dest(dest, tm):
    nt = dest.shape[1] // tm
    return dest.reshape(2, nt, tm).transpose(1, 0, 2).reshape(nt, 1, 2 * tm)


def _dispatch_kernel(zs_ref, zl_ref, dest_ref, *rest, tm, tiles):
    n_src = len(tiles)
    src_refs = rest[:n_src]
    xs_ref, zero_sc, sem, zsem = rest[n_src:]
    i = pl.program_id(0)

    def copy_tile(src_ref):
        def row_copy(g, u, slot):
            group = src_ref.at[pl.ds(pl.multiple_of(g * ROW_GROUP, ROW_GROUP), ROW_GROUP)]
            row = dest_ref[0, 0, slot * tm + g * ROW_GROUP + u]
            return pltpu.make_async_copy(group.at[pl.ds(u, 1)], xs_ref.at[pl.ds(row, 1)], sem)

        def start(g, c):
            for u in range(ROW_GROUP):
                row_copy(g, u, 0).start(priority=0)
                row_copy(g, u, 1).start(priority=1)
            return c

        lax.fori_loop(0, tm // ROW_GROUP, start, 0)

        def wait(g, c):
            for u in range(ROW_GROUP):
                row_copy(g, u, 0).wait()
                row_copy(g, u, 1).wait()
            return c

        lax.fori_loop(0, tm // ROW_GROUP, wait, 0)

    first_tile = 0
    for src_ref, n_tiles in zip(src_refs, tiles):
        @pl.when((i >= first_tile) & (i < first_tile + n_tiles))
        def _(src_ref=src_ref):
            copy_tile(src_ref)
        first_tile += n_tiles

    @pl.when(i == 0)
    def _():
        zero_sc[...] = jnp.zeros(zero_sc.shape, U32)

        def zero_copy(r):
            return pltpu.make_async_copy(zero_sc.at[pl.ds(0, 1)], xs_ref.at[pl.ds(r, 1)], zsem)

        def per_expert(e, c):
            def zstart(r, c2):
                zero_copy(zs_ref[e] + r).start()
                return c2
            lax.fori_loop(0, zl_ref[e], zstart, 0)

            def zwait(r, c2):
                zero_copy(zs_ref[e] + r).wait()
                return c2
            lax.fori_loop(0, zl_ref[e], zwait, 0)
            return c
        lax.fori_loop(0, N_EXPERTS, per_expert, 0)

        def block_copy(b):
            return pltpu.make_async_copy(
                zero_sc, xs_ref.at[pl.ds(pl.multiple_of(b * MOE_ROWS, MOE_ROWS), MOE_ROWS)], zsem)

        n_blocks = xs_ref.shape[0] // MOE_ROWS

        def bstart(b, c):
            block_copy(b).start()
            return c
        lax.fori_loop(zl_ref[N_EXPERTS], n_blocks, bstart, 0)

        def bwait(b, c):
            block_copy(b).wait()
            return c
        lax.fori_loop(zl_ref[N_EXPERTS], n_blocks, bwait, 0)


def _dispatch(zstart, zlen, dests, srcs, n_rows, tm):
    tiles = tuple(s.shape[0] // tm for s in srcs)
    dest3 = jnp.concatenate([_tile_dest(d, tm) for d in dests], axis=0)
    src_specs = []
    first_tile = 0
    for n in tiles:
        src_specs.append(pl.BlockSpec(
            (tm, D_MODEL // 2), lambda i, zs, zl, ft=first_tile, n=n: (jnp.clip(i - ft, 0, n - 1), 0)))
        first_tile += n
    grid_spec = pltpu.PrefetchScalarGridSpec(
        num_scalar_prefetch=2,
        grid=(sum(tiles),),
        in_specs=[pl.BlockSpec((1, 1, 2 * tm), lambda i, zs, zl: (i, 0, 0), memory_space=pltpu.SMEM)]
        + src_specs,
        out_specs=pl.BlockSpec(memory_space=pl.ANY),
        scratch_shapes=[pltpu.VMEM((MOE_ROWS, D_MODEL // 2), U32), pltpu.SemaphoreType.DMA,
                        pltpu.SemaphoreType.DMA],
    )
    return pl.pallas_call(
        functools.partial(_dispatch_kernel, tm=tm, tiles=tiles),
        grid_spec=grid_spec,
        out_shape=jax.ShapeDtypeStruct((n_rows, D_MODEL // 2), U32),
        compiler_params=_cparams(("arbitrary",)),
        name="moe_dispatch",
    )(zstart, zlen, dest3, *srcs)


def _expert_kernel(be_ref, nu_ref, ord_ref, nxt_ref, xs_ref, wg_hbm, wu_hbm, wd_hbm, o_ref,
                   wbuf, wgb, wub, wdb, sem):
    b = pl.program_id(0)
    used = b < nu_ref[0]
    e = be_ref[b]
    new_expert = (b == 0) | (e != be_ref[jnp.maximum(b - 1, 0)])
    slot = ord_ref[b] & 1

    def fetch(expert, s):
        return [pltpu.make_async_copy(w.at[expert], wbuf.at[s, i], sem.at[s])
                for i, w in enumerate((wg_hbm, wu_hbm, wd_hbm))]

    @pl.when((b == 0) & used)
    def _():
        for cp in fetch(e, 0):
            cp.start()

    @pl.when(used & new_expert)
    def _():
        for cp in fetch(e, slot):
            cp.wait()

        @pl.when(nxt_ref[b] >= 0)
        def _():
            for cp in fetch(nxt_ref[b], 1 - slot):
                cp.start()

        rows = 128

        def cast(r, c):
            r0 = pl.multiple_of(r * rows, rows)
            wgb[pl.ds(r0, rows), :] = wbuf[slot, 0, pl.ds(r0, rows), :].astype(BF16)
            wub[pl.ds(r0, rows), :] = wbuf[slot, 1, pl.ds(r0, rows), :].astype(BF16)
            wdb[pl.ds(r0, rows), :] = wbuf[slot, 2, pl.ds(r0, rows), :].astype(BF16)
            return c
        lax.fori_loop(0, D_MODEL // rows, cast, 0)

    @pl.when(used)
    def _():
        lo, hi = _unpack_bf16_pairs(xs_ref[...])
        x = jnp.concatenate([lo, hi], axis=1).astype(BF16)
        g = jnp.dot(x, wgb[...], preferred_element_type=F32)
        u = jnp.dot(x, wub[...], preferred_element_type=F32)
        hmid = (g * jax.nn.sigmoid(g) * u).astype(BF16)
        y = jnp.dot(hmid, wdb[...], preferred_element_type=F32)
        o_ref[...] = _pack_bf16_pairs(y[:, :D_MODEL // 2], y[:, D_MODEL // 2:])

    @pl.when(jnp.logical_not(used))
    def _():
        o_ref[...] = jnp.zeros(o_ref.shape, U32)


def _experts(block_expert, n_used, block_ord, block_next, xs, wg, wu, wd):
    assert EXPERT_FF == D_MODEL
    n_blocks = xs.shape[0] // MOE_ROWS
    row_spec = pl.BlockSpec((MOE_ROWS, D_MODEL // 2), lambda b, *_: (b, 0))
    grid_spec = pltpu.PrefetchScalarGridSpec(
        num_scalar_prefetch=4,
        grid=(n_blocks,),
        in_specs=[row_spec] + [pl.BlockSpec(memory_space=pl.ANY)] * 3,
        out_specs=row_spec,
        scratch_shapes=[pltpu.VMEM((2, 3, D_MODEL, EXPERT_FF), F32),
                        pltpu.VMEM((D_MODEL, EXPERT_FF), BF16), pltpu.VMEM((D_MODEL, EXPERT_FF), BF16),
                        pltpu.VMEM((EXPERT_FF, D_MODEL), BF16), pltpu.SemaphoreType.DMA((2,))],
    )
    return pl.pallas_call(
        _expert_kernel,
        grid_spec=grid_spec,
        out_shape=jax.ShapeDtypeStruct((xs.shape[0], D_MODEL // 2), U32),
        compiler_params=_cparams(("arbitrary",)),
        name="moe_experts",
    )(block_expert, n_used, block_ord, block_next, xs, wg, wu, wd)


def _combine_kernel(dest_ref, ys_ref, h_ref, rw_ref, nw_ref, o_ref, buf, sem, *, tm):
    def row_copy(g, u, slot):
        group = buf.at[slot, pl.ds(pl.multiple_of(g * ROW_GROUP, ROW_GROUP), ROW_GROUP)]
        row = dest_ref[0, 0, slot * tm + g * ROW_GROUP + u]
        return pltpu.make_async_copy(ys_ref.at[pl.ds(row, 1)], group.at[pl.ds(u, 1)], sem)

    def start(g, c):
        for u in range(ROW_GROUP):
            row_copy(g, u, 0).start(priority=0)
            row_copy(g, u, 1).start(priority=1)
        return c

    lax.fori_loop(0, tm // ROW_GROUP, start, 0)

    def wait(g, c):
        for u in range(ROW_GROUP):
            row_copy(g, u, 0).wait()
            row_copy(g, u, 1).wait()
        return c

    lax.fori_loop(0, tm // ROW_GROUP, wait, 0)

    rw = rw_ref[...]
    w1 = rw[:, 0:1]
    w2 = rw[:, 1:2]
    lo1, hi1 = _unpack_bf16_pairs(buf[0])
    lo2, hi2 = _unpack_bf16_pairs(buf[1])
    y = jnp.concatenate([lo1 * w1 + lo2 * w2, hi1 * w1 + hi2 * w2], axis=1)
    h = h_ref[...] + y
    o_ref[...] = h * lax.rsqrt(jnp.mean(h * h, axis=-1, keepdims=True) + NORM_EPS) * nw_ref[...]


def _combine(dest, ys, h, rw, nw, tm):
    te = h.shape[0]
    nt = te // tm
    dest3 = _tile_dest(dest, tm)
    return pl.pallas_call(
        functools.partial(_combine_kernel, tm=tm),
        grid=(nt,),
        in_specs=[
            pl.BlockSpec((1, 1, 2 * tm), lambda i: (i, 0, 0), memory_space=pltpu.SMEM),
            pl.BlockSpec(memory_space=pl.ANY),
            pl.BlockSpec((tm, D_MODEL), lambda i: (i, 0)),
            pl.BlockSpec((tm, ROUTER_LANES), lambda i: (i, 0)),
            pl.BlockSpec((1, D_MODEL), lambda i: (0, 0)),
        ],
        out_specs=pl.BlockSpec((tm, D_MODEL), lambda i: (i, 0)),
        out_shape=jax.ShapeDtypeStruct((te, D_MODEL), F32),
        scratch_shapes=[pltpu.VMEM((2, tm, D_MODEL // 2), U32), pltpu.SemaphoreType.DMA],
        compiler_params=_cparams(("arbitrary",)),
        name="moe_combine",
    )(dest3, ys, h, rw, nw)


def _split_router_weight(w_group, w_expert):
    wrt = jnp.concatenate([w_group.T, w_expert.T], axis=0)
    wrt = jnp.pad(wrt, ((0, ROUTER_LANES - wrt.shape[0]), (0, 0)))
    hi = wrt.astype(BF16)
    lo = (wrt - hi.astype(F32)).astype(BF16)
    return jnp.concatenate([hi, lo], axis=0)


def _rope_tables(seq):
    pos = jnp.arange(seq, dtype=F32)
    inv_freq = ROPE_THETA ** (-jnp.arange(0, HEAD_DIM, 2, dtype=F32) / HEAD_DIM)
    ang = pos[:, None] * inv_freq[None, :]
    c, s = jnp.cos(ang), jnp.sin(ang)
    cos = jnp.concatenate([c, c, c, c], axis=1)
    sin = jnp.concatenate([-s, s, -s, s], axis=1)
    return cos, sin


def _token_mixer(x2d, batch, seq, p, cnt_in):
    cos, sin = _rope_tables(seq)
    f, qz, k, vt = _inproj(x2d, p["nmix"], p["w1"], cos, sin, tm=INPROJ_TOKENS)
    four = _fourier(f, batch, seq)
    attn = _attention(qz, k, vt, p["lam"], p["sub"], batch, seq, tq=ATTN_QUERIES,
                      ck=min(ATTN_KEYS_MAX, seq // 2), sb=ATTN_SUB_KEYS, kv_block=seq)
    return _merge(x2d, four, attn, p["nmix"], p["wgate"], p["bg"], p["wf"], p["wa"], p["wo"],
                  p["nffn"], p["wr"], cnt_in, tm=MERGE_TOKENS)


def kernel(x_prompt, x_sample, norm_mix_w, w_in, b_gate, w_fourier, lambda_q1, lambda_k1, lambda_q2,
           lambda_k2, subln_w, w_attn, w_out, norm_ffn_w, w_group_router, w_expert_router,
           w_expert_gate, w_expert_up, w_expert_down, norm_final_w):
    wr = _split_router_weight(w_group_router[0], w_expert_router[0])
    p = {
        "nmix": norm_mix_w[0][None, :],
        "w1": w_in[0][:, :MIX_WIDTH].astype(BF16),
        "wgate": w_in[0][:, MIX_WIDTH:].astype(BF16),
        "bg": b_gate[0][None, :],
        "wf": w_fourier[0].astype(BF16),
        "wa": w_attn[0].astype(BF16),
        "wo": w_out[0].astype(BF16),
        "nffn": norm_ffn_w[0][None, :],
        "wr": wr,
        "lam": jnp.stack([lambda_q1[0], lambda_k1[0], lambda_q2[0], lambda_k2[0]]),
        "sub": subln_w[0][:, None],
    }
    inputs = [x_prompt, x_sample]
    mixed = []
    cnt = jnp.zeros((1, ROUTER_LANES), F32)
    for x in inputs:
        batch, seq, _ = x.shape
        h, hnp, ri, rw, cnt = _token_mixer(x.reshape(batch * seq, D_MODEL), batch, seq, p, cnt)
        mixed.append((h, hnp, ri, rw))
    outs = _moe_and_final(mixed, cnt, w_expert_gate[0], w_expert_up[0], w_expert_down[0], norm_final_w)
    return tuple(o.reshape(x.shape) for o, x in zip(outs, inputs))


def _moe_and_final(mixed, cnt, w_gate, w_up, w_down, norm_final_w):
    n_assign = 2 * sum(m[0].shape[0] for m in mixed)
    n_blocks = -(-(n_assign + N_EXPERTS * (MOE_ROWS - 1)) // MOE_ROWS)
    n_rows = n_blocks * MOE_ROWS
    counts = cnt[0, EXPERT_LANE0:EXPERT_LANE0 + N_EXPERTS].astype(I32)
    padded = (counts + MOE_ROWS - 1) // MOE_ROWS * MOE_ROWS
    pend = jnp.cumsum(padded)
    pstart = pend - padded
    total = pend[-1]
    bstart = jnp.arange(n_blocks, dtype=I32) * MOE_ROWS
    be = jnp.minimum(jnp.sum((pend[None, :] <= bstart[:, None]).astype(I32), axis=1), N_EXPERTS - 1)
    n_used = (total // MOE_ROWS).astype(I32)
    be = jnp.where(bstart < total, be, be[jnp.maximum(n_used - 1, 0)])
    block_ord = jnp.cumsum(jnp.concatenate([jnp.zeros((1,), I32), (be[1:] != be[:-1]).astype(I32)]))
    eidx = jnp.arange(N_EXPERTS, dtype=I32)
    later_used = (eidx[None, :] > eidx[:, None]) & (padded[None, :] > 0)
    next_expert = jnp.min(jnp.where(later_used, eidx[None, :], N_EXPERTS), axis=1)
    next_expert = jnp.where(next_expert == N_EXPERTS, -1, next_expert)
    block_next = next_expert[be]
    zstart = pstart + counts
    zlen = jnp.concatenate([(pend - zstart).astype(I32), n_used[None]])

    def row_of(expert, rank):
        hit = expert[None].astype(I32) == eidx[:, None, None]
        return jnp.sum(jnp.where(hit, pstart[:, None, None], 0), axis=0) + rank.astype(I32)

    dests = [row_of(m[2][0:2], m[2][2:4]) for m in mixed]
    xs = _dispatch(zstart, zlen, dests, [m[1] for m in mixed], n_rows, tm=DISPATCH_TOKENS)
    ys = _experts(be, n_used[None], block_ord, block_next, xs, w_gate, w_up, w_down)
    return [_combine(dest, ys, m[0], m[3], norm_final_w[None, :], tm=COMBINE_TOKENS)
            for m, dest in zip(mixed, dests)]
```

```python
import functools
import math

import numpy as np
import jax
import jax.numpy as jnp
from jax import lax
from jax.experimental import pallas as pl
from jax.experimental.pallas import tpu as pltpu

F32 = jnp.float32
BF16 = jnp.bfloat16
U32 = jnp.uint32
I32 = jnp.int32

D_MODEL = 1024
FOURIER_WIDTH = 512
FOURIER_GROUPS = 4
GROUP_DIM = 128
HEADS = 4
HEAD_DIM = 64
QK_WIDTH = 512
V_DIM = 128
V_WIDTH = 512
ROPE_THETA = 10000.0
N_GROUPS = 4
EXPERTS_PER_GROUP = 8
N_EXPERTS = 32
EXPERT_FF = 1024
NORM_EPS = 1e-6
LAMBDA_INIT = 0.8 - 0.6 * math.exp(0.0)

LANES = 128
ROUTER_LANES = 128
EXPERT_LANE0 = N_GROUPS
MOE_ROWS = 256
ROW_GROUP = 8
HEAD_SHIFT = HEADS.bit_length() - 1
MIX_WIDTH = FOURIER_WIDTH + 2 * QK_WIDTH + V_WIDTH

INPROJ_TOKENS = 512
ATTN_QUERIES = 512
ATTN_KEYS_MAX = 4096
ATTN_SUB_KEYS = 256
MERGE_TOKENS = 512
DISPATCH_TOKENS = 1024
COMBINE_TOKENS = 512
DFT_N = 128
VMEM_LIMIT = 56 * 1024 * 1024

LOG2E = 1.4426950408889634
NEG_BIG = -3.0e38


def _cparams(sem, vmem=VMEM_LIMIT, flags=None):
    return pltpu.CompilerParams(dimension_semantics=sem, vmem_limit_bytes=vmem, flags=flags)


def _swap_halves(t):
    n = t.shape[1]
    lane = lax.broadcasted_iota(I32, t.shape, 1)
    first = (lane % HEAD_DIM) < (HEAD_DIM // 2)
    return jnp.where(first, pltpu.roll(t, n - HEAD_DIM // 2, 1), pltpu.roll(t, HEAD_DIM // 2, 1))


def _inproj_kernel(x_ref, nw_ref, w_ref, cos_ref, sin_ref, f_ref, qz_ref, k_ref, vt_ref):
    x = x_ref[...]
    xn = x * lax.rsqrt(jnp.mean(x * x, axis=-1, keepdims=True) + NORM_EPS) * nw_ref[...]
    proj = jnp.dot(xn.astype(BF16), w_ref[...], preferred_element_type=F32)
    f_ref[...] = proj[:, :FOURIER_WIDTH].astype(BF16)
    q = proj[:, FOURIER_WIDTH:FOURIER_WIDTH + QK_WIDTH]
    k = proj[:, FOURIER_WIDTH + QK_WIDTH:FOURIER_WIDTH + 2 * QK_WIDTH]
    v = proj[:, FOURIER_WIDTH + 2 * QK_WIDTH:]
    cos = jnp.concatenate([cos_ref[...]] * (QK_WIDTH // LANES), axis=1)
    sin = jnp.concatenate([sin_ref[...]] * (QK_WIDTH // LANES), axis=1)
    q = q * cos + _swap_halves(q) * sin
    k = k * cos + _swap_halves(k) * sin
    q = q * (HEAD_DIM ** -0.5 * LOG2E)
    lane = lax.broadcasted_iota(I32, (q.shape[0], LANES), 1)
    lo = lane < HEAD_DIM
    vt = v.T.astype(BF16)
    for h in range(HEADS):
        qh = q[:, h * LANES:(h + 1) * LANES]
        qz_ref[2 * h] = jnp.where(lo, qh, 0.0).astype(BF16)
        qz_ref[2 * h + 1] = jnp.where(lo, 0.0, qh).astype(BF16)
        k_ref[h] = k[:, h * LANES:(h + 1) * LANES].astype(BF16)
        vt_ref[0, h] = vt[h * V_DIM:(h + 1) * V_DIM, :]


def _inproj(x, nw, w1, cos, sin, tm):
    te = x.shape[0]
    nt = te // tm
    pos_tiles = cos.shape[0] // tm
    return pl.pallas_call(
        _inproj_kernel,
        grid=(nt,),
        in_specs=[
            pl.BlockSpec((tm, D_MODEL), lambda i: (i, 0)),
            pl.BlockSpec((1, D_MODEL), lambda i: (0, 0)),
            pl.BlockSpec((D_MODEL, MIX_WIDTH), lambda i: (0, 0)),
            pl.BlockSpec((tm, LANES), lambda i: (i % pos_tiles, 0)),
            pl.BlockSpec((tm, LANES), lambda i: (i % pos_tiles, 0)),
        ],
        out_specs=[
            pl.BlockSpec((tm, FOURIER_WIDTH), lambda i: (i, 0)),
            pl.BlockSpec((2 * HEADS, tm, LANES), lambda i: (0, i, 0)),
            pl.BlockSpec((HEADS, tm, LANES), lambda i: (0, i, 0)),
            pl.BlockSpec((1, HEADS, V_DIM, tm), lambda i: (i, 0, 0, 0)),
        ],
        out_shape=[
            jax.ShapeDtypeStruct((te, FOURIER_WIDTH), BF16),
            jax.ShapeDtypeStruct((2 * HEADS, te, LANES), BF16),
            jax.ShapeDtypeStruct((HEADS, te, LANES), BF16),
            jax.ShapeDtypeStruct((nt, HEADS, V_DIM, tm), BF16),
        ],
        compiler_params=_cparams(("parallel",)),
        name="inproj",
    )(x, nw, w1, cos, sin)


@functools.lru_cache(maxsize=None)
def _dft_tables(batch, seq):
    n1 = seq // DFT_N
    assert batch * n1 == DFT_N
    idx = np.arange(n1)
    ang = 2.0 * np.pi * ((idx[:, None] * idx[None, :]) % n1) / n1
    eye = np.eye(batch)
    m1 = np.concatenate([np.kron(eye, np.cos(ang)), -np.kron(eye, np.sin(ang))], axis=0)
    k1 = np.arange(n1)[:, None, None]
    k2 = np.arange(DFT_N)[None, :, None]
    n2 = np.arange(DFT_N)[None, None, :]
    ang2 = 2.0 * np.pi * ((n2 * (k1 + n1 * k2)) % seq) / seq
    er, ei = np.cos(ang2), -np.sin(ang2)
    m2 = np.concatenate([np.concatenate([er, -ei], axis=2), np.concatenate([ei, er], axis=2)], axis=1)
    c = np.arange(GROUP_DIM)
    ang3 = 2.0 * np.pi * ((c[:, None] * c[None, :]) % GROUP_DIM) / GROUP_DIM
    norm = 1.0 / math.sqrt(seq * GROUP_DIM)
    cs = np.concatenate([np.cos(ang3), np.sin(ang3)], axis=0) * norm
    return (jnp.asarray(m1, BF16), jnp.asarray(m2, BF16), jnp.asarray(cs, BF16))


def _dft1_kernel(m_ref, x_ref, y_ref):
    y_ref[...] = jnp.dot(m_ref[...], x_ref[...], preferred_element_type=F32).astype(BF16)


def _dft1(m1, x2d, tn=4096):
    n = x2d.shape[1]
    return pl.pallas_call(
        _dft1_kernel,
        grid=(n // tn,),
        in_specs=[pl.BlockSpec((2 * DFT_N, DFT_N), lambda i: (0, 0)),
                  pl.BlockSpec((DFT_N, tn), lambda i: (0, i))],
        out_specs=pl.BlockSpec((2 * DFT_N, tn), lambda i: (0, i)),
        out_shape=jax.ShapeDtypeStruct((2 * DFT_N, n), BF16),
        compiler_params=_cparams(("parallel",)),
        name="dft_stage1",
    )(m1, x2d)


def _dft2_kernel(m_ref, yr_ref, yi_ref, cs_ref, o_ref, *, rows_per_step):
    for j in range(rows_per_step):
        m = m_ref[j]
        z = (jnp.dot(m[:, :DFT_N], yr_ref[0, j], preferred_element_type=F32)
             + jnp.dot(m[:, DFT_N:], yi_ref[0, j], preferred_element_type=F32))
        zb = z.astype(BF16)
        for g in range(FOURIER_GROUPS):
            zg = jnp.concatenate([zb[:DFT_N, g * LANES:(g + 1) * LANES],
                                  zb[DFT_N:, g * LANES:(g + 1) * LANES]], axis=1)
            o_ref[0, :, (j * FOURIER_GROUPS + g) * LANES:(j * FOURIER_GROUPS + g + 1) * LANES] = (
                jnp.dot(zg, cs_ref[...], preferred_element_type=F32).astype(BF16))


def _dft2(m2, y, cs, batch, seq, rows_per_step=4):
    n1 = seq // DFT_N
    r = rows_per_step
    steps = DFT_N // r
    per_b = n1 // r
    return pl.pallas_call(
        functools.partial(_dft2_kernel, rows_per_step=r),
        grid=(steps,),
        in_specs=[
            pl.BlockSpec((r, 2 * DFT_N, 2 * DFT_N), lambda i: (i % per_b, 0, 0)),
            pl.BlockSpec((1, r, DFT_N, FOURIER_WIDTH), lambda i: (0, i, 0, 0)),
            pl.BlockSpec((1, r, DFT_N, FOURIER_WIDTH), lambda i: (1, i, 0, 0)),
            pl.BlockSpec((2 * GROUP_DIM, GROUP_DIM), lambda i: (0, 0)),
        ],
        out_specs=pl.BlockSpec((1, DFT_N, r * FOURIER_WIDTH), lambda i: (i // per_b, 0, i % per_b)),
        out_shape=jax.ShapeDtypeStruct((batch, DFT_N, n1 * FOURIER_WIDTH), BF16),
        compiler_params=_cparams(("parallel",)),
        name="dft_stage2",
    )(m2, y, y, cs)


def _fourier(f, batch, seq):
    m1, m2, cs = _dft_tables(batch, seq)
    y = _dft1(m1, f.reshape(DFT_N, DFT_N * FOURIER_WIDTH))
    out = _dft2(m2, y.reshape(2, DFT_N, DFT_N, FOURIER_WIDTH), cs, batch, seq)
    return out.reshape(batch * seq, FOURIER_WIDTH)


def _attn_kernel(qz_ref, k_ref, vt_ref, lam_ref, sub_ref, o_ref,
                 s_a, s_b, mx_a, mx_b, m_a, m_b, l_a, l_b, acc_a, acc_b, *, ck, sb, n_chunks):
    j = pl.program_id(2)
    n_items = n_chunks * HEADS

    @pl.when(j == 0)
    def _():
        for m_sc, l_sc, acc_sc in ((m_a, l_a, acc_a), (m_b, l_b, acc_b)):
            m_sc[...] = jnp.full(m_sc.shape, NEG_BIG, F32)
            l_sc[...] = jnp.zeros(l_sc.shape, F32)
            acc_sc[...] = jnp.zeros(acc_sc.shape, F32)

    def split(t):
        t = jnp.asarray(t, I32)
        return lax.shift_right_logical(t, HEAD_SHIFT), t & (HEADS - 1)

    ckv = vt_ref.shape[-1]
    n_sub = ck // sb
    bufs = ((s_a, mx_a, m_a, l_a, acc_a), (s_b, mx_b, m_b, l_b, acc_b))

    def step(t, do_update, do_scores):
        if do_update:
            c, h = split(t)
        if do_scores:
            c1, h1 = split(t + 1)
        state = []
        for cc, (s_ref, mx_ref, m_sc, l_sc, acc_sc) in enumerate(bufs):
            if do_update:
                m_old = m_sc[h]
                m_new = jnp.maximum(m_old, mx_ref[...])
                state.append([m_old, m_new, None, None, None])
            else:
                state.append([None, None, None, None, None])
        for u in range(n_sub):
            rows = pl.ds(u * sb, sb)
            for cc, (s_ref, mx_ref, m_sc, l_sc, acc_sc) in enumerate(bufs):
                st = state[cc]
                if do_update:
                    p = jnp.exp2(s_ref[rows, :] - st[1])
                    ps = jnp.sum(p, axis=0, keepdims=True)
                    lane0 = (u * sb) % ckv
                    vth = vt_ref[c * (ck // ckv) + (u * sb) // ckv, h, :, lane0:lane0 + sb]
                    d = jnp.dot(vth, p.astype(BF16), preferred_element_type=F32)
                    st[2] = ps if st[2] is None else st[2] + ps
                    st[3] = d if st[3] is None else st[3] + d
                if do_scores:
                    row0 = pl.multiple_of(c1 * ck + u * sb, sb)
                    s = lax.dot_general(k_ref[h1, pl.ds(row0, sb), :], qz_ref[2 * h1 + cc],
                                        (((1,), (1,)), ((), ())), preferred_element_type=F32)
                    s_ref[rows, :] = s
                    mx = jnp.max(s, axis=0, keepdims=True)
                    st[4] = mx if st[4] is None else jnp.maximum(st[4], mx)
        for cc, (s_ref, mx_ref, m_sc, l_sc, acc_sc) in enumerate(bufs):
            m_old, m_new, l_add, pv, mx = state[cc]
            if do_update:
                alpha = jnp.exp2(m_old - m_new)
                l_sc[h] = alpha * l_sc[h] + l_add
                acc_sc[h] = alpha * acc_sc[h] + pv
                m_sc[h] = m_new
            if do_scores:
                mx_ref[...] = mx

    step(-1, False, True)

    def body(t, carry):
        step(t, True, True)
        return carry

    lax.fori_loop(0, n_items - 1, body, 0)
    step(n_items - 1, True, False)

    @pl.when(j == pl.num_programs(2) - 1)
    def _():
        lv = lam_ref[...]
        lam = (jnp.exp(jnp.sum(lv[0:1] * lv[1:2], axis=1, keepdims=True))
               - jnp.exp(jnp.sum(lv[2:3] * lv[3:4], axis=1, keepdims=True)) + LAMBDA_INIT)
        for h in range(HEADS):
            o = acc_a[h] / l_a[h] - lam * (acc_b[h] / l_b[h])
            o = o * lax.rsqrt(jnp.mean(o * o, axis=0, keepdims=True) + NORM_EPS)
            o = o * sub_ref[...] * (1.0 - LAMBDA_INIT)
            o_ref[:, h * V_DIM:(h + 1) * V_DIM] = o.T.astype(BF16)


def _attention(qz, k, vt, lam_vecs, sub_col, batch, seq, tq, ck, sb, kv_block):
    nq = seq // tq
    nkb = seq // kv_block
    n_chunks = kv_block // ck
    ckv = vt.shape[-1]
    stat = pltpu.VMEM((HEADS, 1, tq), F32)
    kv_mode = dict(pipeline_mode=pl.Buffered(1)) if batch * nkb == 1 else {}
    return pl.pallas_call(
        functools.partial(_attn_kernel, ck=ck, sb=sb, n_chunks=n_chunks),
        grid=(batch, nq, nkb),
        in_specs=[
            pl.BlockSpec((2 * HEADS, tq, LANES), lambda b, i, j: (0, b * nq + i, 0)),
            pl.BlockSpec((HEADS, kv_block, LANES), lambda b, i, j: (0, b * nkb + j, 0), **kv_mode),
            pl.BlockSpec((kv_block // ckv, HEADS, V_DIM, ckv), lambda b, i, j: (b * nkb + j, 0, 0, 0),
                         **kv_mode),
            pl.BlockSpec((4, HEAD_DIM), lambda b, i, j: (0, 0)),
            pl.BlockSpec((V_DIM, 1), lambda b, i, j: (0, 0)),
        ],
        out_specs=pl.BlockSpec((tq, V_WIDTH), lambda b, i, j: (b * nq + i, 0)),
        out_shape=jax.ShapeDtypeStruct((batch * seq, V_WIDTH), BF16),
        scratch_shapes=[
            pltpu.VMEM((ck, tq), F32), pltpu.VMEM((ck, tq), F32),
            pltpu.VMEM((1, tq), F32), pltpu.VMEM((1, tq), F32),
            stat, stat, stat, stat,
            pltpu.VMEM((HEADS, V_DIM, tq), F32), pltpu.VMEM((HEADS, V_DIM, tq), F32),
        ],
        compiler_params=_cparams(("parallel", "parallel", "arbitrary")),
        name="diff_attention",
    )(qz, k, vt, lam_vecs, sub_col)


def _pack_bf16_pairs(a, b):
    ab = lax.bitcast_convert_type(a.astype(BF16).astype(F32), U32)
    bb = lax.bitcast_convert_type(b.astype(BF16).astype(F32), U32)
    return (bb & jnp.uint32(0xFFFF0000)) | (ab >> 16)


def _unpack_bf16_pairs(w):
    lo = lax.bitcast_convert_type(w << 16, F32)
    hi = lax.bitcast_convert_type(w & jnp.uint32(0xFFFF0000), F32)
    return lo, hi


def _merge_kernel(x_ref, four_ref, attn_ref, nmix_ref, wg_ref, bg_ref, wf_ref, wa_ref, wo_ref,
                  nffn_ref, wr_ref, cnt_in_ref,
                  h_ref, hnp_ref, ri_ref, rw_ref, cnt_ref, carry_sc, hn_sc):
    i = pl.program_id(0)

    @pl.when(i == 0)
    def _():
        carry_sc[...] = cnt_in_ref[...]
        hn_sc[...] = jnp.zeros(hn_sc.shape, F32)

    st = {}
    tm = hn_sc.shape[0]
    lane = lax.broadcasted_iota(I32, (tm, ROUTER_LANES), 1)

    def r_logits():
        hn = hn_sc[...]
        hn_hi = hn.astype(BF16)
        hn_lo = (hn - hn_hi.astype(F32)).astype(BF16)
        parts = lax.dot_general(jnp.concatenate([hn_hi, hn_lo], axis=0), wr_ref[...],
                                (((1,), (1,)), ((), ())), preferred_element_type=F32)
        st["lg"] = (((parts[tm:, ROUTER_LANES:] + parts[:tm, ROUTER_LANES:])
                     + parts[tm:, :ROUTER_LANES]) + parts[:tm, :ROUTER_LANES])

    def r_group():
        gmask = lane < N_GROUPS
        gl = jnp.where(gmask, st["lg"], NEG_BIG)
        gmax = jnp.max(gl, axis=1, keepdims=True)
        st["g_idx"] = jnp.min(jnp.where(gl == gmax, lane, ROUTER_LANES), axis=1, keepdims=True)
        st["g_w"] = 1.0 / jnp.sum(jnp.where(gmask, jnp.exp(gl - gmax), 0.0), axis=1, keepdims=True)

    def r_top1():
        e_lane = lane - EXPERT_LANE0
        emask = ((e_lane >= 0) & (e_lane < N_EXPERTS)
                 & ((e_lane // EXPERTS_PER_GROUP) == st["g_idx"]))
        el = jnp.where(emask, st["lg"], NEG_BIG)
        st["v1"] = jnp.max(el, axis=1, keepdims=True)
        st["i1"] = jnp.min(jnp.where(el == st["v1"], lane, ROUTER_LANES), axis=1, keepdims=True)
        st["el"] = el

    def r_top2():
        el2 = jnp.where(lane == st["i1"], NEG_BIG, st["el"])
        v2 = jnp.max(el2, axis=1, keepdims=True)
        st["i2"] = jnp.min(jnp.where(el2 == v2, lane, ROUTER_LANES), axis=1, keepdims=True)
        s21 = jnp.exp(v2 - st["v1"])
        w1 = st["g_w"] / (1.0 + s21)
        w2 = st["g_w"] * s21 / (1.0 + s21)
        rw_ref[...] = jnp.where(lane == 0, w1, jnp.where(lane == 1, w2, 0.0))

    def r_rank():
        oh1 = lane == st["i1"]
        oh2 = lane == st["i2"]
        ohs = (oh1 | oh2).astype(BF16)
        row = lax.broadcasted_iota(I32, (tm, tm), 0)
        col = lax.broadcasted_iota(I32, (tm, tm), 1)
        tril = (col < row).astype(BF16)
        before = jnp.dot(tril, ohs, preferred_element_type=F32) + carry_sc[...]
        st["r1"] = jnp.sum(jnp.where(oh1, before, 0.0), axis=1, keepdims=True)
        st["r2"] = jnp.sum(jnp.where(oh2, before, 0.0), axis=1, keepdims=True)
        real_tile = (i > 0).astype(F32)
        carry_sc[...] = carry_sc[...] + real_tile * jnp.sum(ohs.astype(F32), axis=0, keepdims=True)
        cnt_ref[...] = carry_sc[...]

    def r_store():
        e1 = (st["i1"] - EXPERT_LANE0).astype(F32)
        e2 = (st["i2"] - EXPERT_LANE0).astype(F32)
        ri = jnp.where(lane == 0, e1, jnp.where(lane == 1, e2,
             jnp.where(lane == 2, st["r1"], jnp.where(lane == 3, st["r2"], 0.0))))
        ri_ref[...] = ri.T

    gate_cols = 512

    def m_norm():
        x = x_ref[...]
        xn = x * lax.rsqrt(jnp.mean(x * x, axis=-1, keepdims=True) + NORM_EPS) * nmix_ref[...]
        st["xnb"] = xn.astype(BF16)
        st["gates"] = []

    def m_gate(c):
        cols = slice(c * gate_cols, (c + 1) * gate_cols)
        st["gates"].append(jax.nn.sigmoid(
            jnp.dot(st["xnb"], wg_ref[:, cols], preferred_element_type=F32) + bg_ref[:, cols]))

    def m_fourier():
        st["bf"] = jnp.dot(four_ref[...], wf_ref[...], preferred_element_type=F32)

    def m_attn():
        st["ba"] = jnp.dot(attn_ref[...], wa_ref[...], preferred_element_type=F32)

    def m_merge():
        gates = jnp.concatenate(st["gates"], axis=1)
        st["merged"] = (gates[:, :D_MODEL] * st["bf"] + gates[:, D_MODEL:] * st["ba"]).astype(BF16)

    def m_out():
        h = x_ref[...] + jnp.dot(st["merged"], wo_ref[...], preferred_element_type=F32)
        h_ref[...] = h
        hn_next = h * lax.rsqrt(jnp.mean(h * h, axis=-1, keepdims=True) + NORM_EPS) * nffn_ref[...]
        hnp_ref[...] = _pack_bf16_pairs(hn_next[:, :D_MODEL // 2], hn_next[:, D_MODEL // 2:])
        hn_sc[...] = hn_next

    n_gate = 2 * D_MODEL // gate_cols
    mixing = ([m_norm] + [functools.partial(m_gate, c) for c in range(n_gate)]
              + [m_fourier, m_attn, m_merge, m_out])
    routing = [r_logits, r_group, r_top1, r_top2, r_rank, r_store]
    for k in range(max(len(mixing), len(routing))):
        if k < len(routing):
            routing[k]()
        if k < len(mixing):
            mixing[k]()


def _merge(x, four, attn, nmix, wg, bg, wf, wa, wo, nffn, wr, cnt_in, tm):
    te = x.shape[0]
    nt = te // tm
    const = lambda i: (0, 0)
    tile = lambda i: (jnp.minimum(i, nt - 1), 0)
    prev = lambda i: (jnp.maximum(i - 1, 0), 0)
    return pl.pallas_call(
        _merge_kernel,
        grid=(nt + 1,),
        in_specs=[
            pl.BlockSpec((tm, D_MODEL), tile),
            pl.BlockSpec((tm, FOURIER_WIDTH), tile),
            pl.BlockSpec((tm, V_WIDTH), tile),
            pl.BlockSpec((1, D_MODEL), const),
            pl.BlockSpec((D_MODEL, 2 * D_MODEL), const),
            pl.BlockSpec((1, 2 * D_MODEL), const),
            pl.BlockSpec((FOURIER_WIDTH, D_MODEL), const),
            pl.BlockSpec((V_WIDTH, D_MODEL), const),
            pl.BlockSpec((D_MODEL, D_MODEL), const),
            pl.BlockSpec((1, D_MODEL), const),
            pl.BlockSpec((2 * ROUTER_LANES, D_MODEL), const),
            pl.BlockSpec((1, ROUTER_LANES), const),
        ],
        out_specs=[
            pl.BlockSpec((tm, D_MODEL), tile),
            pl.BlockSpec((tm, D_MODEL // 2), tile),
            pl.BlockSpec((ROUTER_LANES, tm), lambda i: (0, jnp.maximum(i - 1, 0))),
            pl.BlockSpec((tm, ROUTER_LANES), prev),
            pl.BlockSpec((1, ROUTER_LANES), const),
        ],
        out_shape=[
            jax.ShapeDtypeStruct((te, D_MODEL), F32),
            jax.ShapeDtypeStruct((te, D_MODEL // 2), U32),
            jax.ShapeDtypeStruct((ROUTER_LANES, te), F32),
            jax.ShapeDtypeStruct((te, ROUTER_LANES), F32),
            jax.ShapeDtypeStruct((1, ROUTER_LANES), F32),
        ],
        scratch_shapes=[pltpu.VMEM((1, ROUTER_LANES), F32), pltpu.VMEM((tm, D_MODEL), F32)],
        compiler_params=_cparams(("arbitrary",)),
        name="merge_router",
    )(x, four, attn, nmix, wg, bg, wf, wa, wo, nffn, wr, cnt_in)


def _tile_dest(dest, tm):
    nt = dest.shape[1] // tm
    return dest.reshape(2, nt, tm).transpose(1, 0, 2).reshape(nt, 1, 2 * tm)


def _dispatch_kernel(zs_ref, zl_ref, dest_ref, *rest, tm, tiles):
    n_src = len(tiles)
    src_refs = rest[:n_src]
    xs_ref, zero_sc, sem, zsem = rest[n_src:]
    i = pl.program_id(0)

    def copy_tile(src_ref):
        def row_copy(g, u, slot):
            group = src_ref.at[pl.ds(pl.multiple_of(g * ROW_GROUP, ROW_GROUP), ROW_GROUP)]
            row = dest_ref[0, 0, slot * tm + g * ROW_GROUP + u]
            return pltpu.make_async_copy(group.at[pl.ds(u, 1)], xs_ref.at[pl.ds(row, 1)], sem)

        def start(g, c):
            for u in range(ROW_GROUP):
                row_copy(g, u, 0).start(priority=0)
                row_copy(g, u, 1).start(priority=1)
            return c

        lax.fori_loop(0, tm // ROW_GROUP, start, 0)

        def wait(g, c):
            for u in range(ROW_GROUP):
                row_copy(g, u, 0).wait()
                row_copy(g, u, 1).wait()
            return c

        lax.fori_loop(0, tm // ROW_GROUP, wait, 0)

    first_tile = 0
    for src_ref, n_tiles in zip(src_refs, tiles):
        @pl.when((i >= first_tile) & (i < first_tile + n_tiles))
        def _(src_ref=src_ref):
            copy_tile(src_ref)
        first_tile += n_tiles

    @pl.when(i == 0)
    def _():
        zero_sc[...] = jnp.zeros(zero_sc.shape, U32)

        def zero_copy(r):
            return pltpu.make_async_copy(zero_sc.at[pl.ds(0, 1)], xs_ref.at[pl.ds(r, 1)], zsem)

        def per_expert(e, c):
            def zstart(r, c2):
                zero_copy(zs_ref[e] + r).start()
                return c2
            lax.fori_loop(0, zl_ref[e], zstart, 0)

            def zwait(r, c2):
                zero_copy(zs_ref[e] + r).wait()
                return c2
            lax.fori_loop(0, zl_ref[e], zwait, 0)
            return c
        lax.fori_loop(0, N_EXPERTS, per_expert, 0)

        def block_copy(b):
            return pltpu.make_async_copy(
                zero_sc, xs_ref.at[pl.ds(pl.multiple_of(b * MOE_ROWS, MOE_ROWS), MOE_ROWS)], zsem)

        n_blocks = xs_ref.shape[0] // MOE_ROWS

        def bstart(b, c):
            block_copy(b).start()
            return c
        lax.fori_loop(zl_ref[N_EXPERTS], n_blocks, bstart, 0)

        def bwait(b, c):
            block_copy(b).wait()
            return c
        lax.fori_loop(zl_ref[N_EXPERTS], n_blocks, bwait, 0)


def _dispatch(zstart, zlen, dests, srcs, n_rows, tm):
    tiles = tuple(s.shape[0] // tm for s in srcs)
    dest3 = jnp.concatenate([_tile_dest(d, tm) for d in dests], axis=0)
    src_specs = []
    first_tile = 0
    for n in tiles:
        src_specs.append(pl.BlockSpec(
            (tm, D_MODEL // 2), lambda i, zs, zl, ft=first_tile, n=n: (jnp.clip(i - ft, 0, n - 1), 0)))
        first_tile += n
    grid_spec = pltpu.PrefetchScalarGridSpec(
        num_scalar_prefetch=2,
        grid=(sum(tiles),),
        in_specs=[pl.BlockSpec((1, 1, 2 * tm), lambda i, zs, zl: (i, 0, 0), memory_space=pltpu.SMEM)]
        + src_specs,
        out_specs=pl.BlockSpec(memory_space=pl.ANY),
        scratch_shapes=[pltpu.VMEM((MOE_ROWS, D_MODEL // 2), U32), pltpu.SemaphoreType.DMA,
                        pltpu.SemaphoreType.DMA],
    )
    return pl.pallas_call(
        functools.partial(_dispatch_kernel, tm=tm, tiles=tiles),
        grid_spec=grid_spec,
        out_shape=jax.ShapeDtypeStruct((n_rows, D_MODEL // 2), U32),
        compiler_params=_cparams(("arbitrary",)),
        name="moe_dispatch",
    )(zstart, zlen, dest3, *srcs)


def _expert_kernel(be_ref, nu_ref, ord_ref, nxt_ref, xs_ref, wg_hbm, wu_hbm, wd_hbm, o_ref,
                   wbuf, wgb, wub, wdb, sem):
    b = pl.program_id(0)
    used = b < nu_ref[0]
    e = be_ref[b]
    new_expert = (b == 0) | (e != be_ref[jnp.maximum(b - 1, 0)])
    slot = ord_ref[b] & 1

    def fetch(expert, s):
        return [pltpu.make_async_copy(w.at[expert], wbuf.at[s, i], sem.at[s])
                for i, w in enumerate((wg_hbm, wu_hbm, wd_hbm))]

    @pl.when((b == 0) & used)
    def _():
        for cp in fetch(e, 0):
            cp.start()

    @pl.when(used & new_expert)
    def _():
        for cp in fetch(e, slot):
            cp.wait()

        @pl.when(nxt_ref[b] >= 0)
        def _():
            for cp in fetch(nxt_ref[b], 1 - slot):
                cp.start()

        rows = 128

        def cast(r, c):
            r0 = pl.multiple_of(r * rows, rows)
            wgb[pl.ds(r0, rows), :] = wbuf[slot, 0, pl.ds(r0, rows), :].astype(BF16)
            wub[pl.ds(r0, rows), :] = wbuf[slot, 1, pl.ds(r0, rows), :].astype(BF16)
            wdb[pl.ds(r0, rows), :] = wbuf[slot, 2, pl.ds(r0, rows), :].astype(BF16)
            return c
        lax.fori_loop(0, D_MODEL // rows, cast, 0)

    @pl.when(used)
    def _():
        lo, hi = _unpack_bf16_pairs(xs_ref[...])
        x = jnp.concatenate([lo, hi], axis=1).astype(BF16)
        g = jnp.dot(x, wgb[...], preferred_element_type=F32)
        u = jnp.dot(x, wub[...], preferred_element_type=F32)
        hmid = (g * jax.nn.sigmoid(g) * u).astype(BF16)
        y = jnp.dot(hmid, wdb[...], preferred_element_type=F32)
        o_ref[...] = _pack_bf16_pairs(y[:, :D_MODEL // 2], y[:, D_MODEL // 2:])

    @pl.when(jnp.logical_not(used))
    def _():
        o_ref[...] = jnp.zeros(o_ref.shape, U32)


def _experts(block_expert, n_used, block_ord, block_next, xs, wg, wu, wd):
    assert EXPERT_FF == D_MODEL
    n_blocks = xs.shape[0] // MOE_ROWS
    row_spec = pl.BlockSpec((MOE_ROWS, D_MODEL // 2), lambda b, *_: (b, 0))
    grid_spec = pltpu.PrefetchScalarGridSpec(
        num_scalar_prefetch=4,
        grid=(n_blocks,),
        in_specs=[row_spec] + [pl.BlockSpec(memory_space=pl.ANY)] * 3,
        out_specs=row_spec,
        scratch_shapes=[pltpu.VMEM((2, 3, D_MODEL, EXPERT_FF), F32),
                        pltpu.VMEM((D_MODEL, EXPERT_FF), BF16), pltpu.VMEM((D_MODEL, EXPERT_FF), BF16),
                        pltpu.VMEM((EXPERT_FF, D_MODEL), BF16), pltpu.SemaphoreType.DMA((2,))],
    )
    return pl.pallas_call(
        _expert_kernel,
        grid_spec=grid_spec,
        out_shape=jax.ShapeDtypeStruct((xs.shape[0], D_MODEL // 2), U32),
        compiler_params=_cparams(("arbitrary",)),
        name="moe_experts",
    )(block_expert, n_used, block_ord, block_next, xs, wg, wu, wd)


def _combine_kernel(dest_ref, dest_next_ref, ys_ref, h_ref, rw_ref, nw_ref, o_ref, buf, sem, *, tm, nt):
    i = pl.program_id(0)
    half = i & 1

    def row_copy(idx_ref, par, g, u, slot):
        group = buf.at[par, slot, pl.ds(pl.multiple_of(g * ROW_GROUP, ROW_GROUP), ROW_GROUP)]
        row = idx_ref[0, 0, slot * tm + g * ROW_GROUP + u]
        return pltpu.make_async_copy(ys_ref.at[pl.ds(row, 1)], group.at[pl.ds(u, 1)], sem.at[par])

    def start_tile(idx_ref, par):
        def start(g, c):
            for u in range(ROW_GROUP):
                row_copy(idx_ref, par, g, u, 0).start(priority=0)
                row_copy(idx_ref, par, g, u, 1).start(priority=1)
            return c
        lax.fori_loop(0, tm // ROW_GROUP, start, 0)

    @pl.when(i == 0)
    def _():
        start_tile(dest_ref, 0)

    @pl.when(i + 1 < nt)
    def _():
        start_tile(dest_next_ref, 1 - half)

    def wait(g, c):
        for u in range(ROW_GROUP):
            row_copy(dest_ref, half, g, u, 0).wait()
            row_copy(dest_ref, half, g, u, 1).wait()
        return c

    lax.fori_loop(0, tm // ROW_GROUP, wait, 0)

    rw = rw_ref[...]
    w1 = rw[:, 0:1]
    w2 = rw[:, 1:2]
    lo1, hi1 = _unpack_bf16_pairs(buf[half, 0])
    lo2, hi2 = _unpack_bf16_pairs(buf[half, 1])
    y = jnp.concatenate([lo1 * w1 + lo2 * w2, hi1 * w1 + hi2 * w2], axis=1)
    h = h_ref[...] + y
    o_ref[...] = h * lax.rsqrt(jnp.mean(h * h, axis=-1, keepdims=True) + NORM_EPS) * nw_ref[...]


def _combine(dest, ys, h, rw, nw, tm):
    te = h.shape[0]
    nt = te // tm
    dest3 = _tile_dest(dest, tm)
    return pl.pallas_call(
        functools.partial(_combine_kernel, tm=tm, nt=nt),
        grid=(nt,),
        in_specs=[
            pl.BlockSpec((1, 1, 2 * tm), lambda i: (i, 0, 0), memory_space=pltpu.SMEM),
            pl.BlockSpec((1, 1, 2 * tm), lambda i: (jnp.minimum(i + 1, nt - 1), 0, 0),
                         memory_space=pltpu.SMEM),
            pl.BlockSpec(memory_space=pl.ANY),
            pl.BlockSpec((tm, D_MODEL), lambda i: (i, 0)),
            pl.BlockSpec((tm, ROUTER_LANES), lambda i: (i, 0)),
            pl.BlockSpec((1, D_MODEL), lambda i: (0, 0)),
        ],
        out_specs=pl.BlockSpec((tm, D_MODEL), lambda i: (i, 0)),
        out_shape=jax.ShapeDtypeStruct((te, D_MODEL), F32),
        scratch_shapes=[pltpu.VMEM((2, 2, tm, D_MODEL // 2), U32), pltpu.SemaphoreType.DMA((2,))],
        compiler_params=_cparams(("arbitrary",)),
        name="moe_combine",
    )(dest3, dest3, ys, h, rw, nw)


def _split_router_weight(w_group, w_expert):
    wrt = jnp.concatenate([w_group.T, w_expert.T], axis=0)
    wrt = jnp.pad(wrt, ((0, ROUTER_LANES - wrt.shape[0]), (0, 0)))
    hi = wrt.astype(BF16)
    lo = (wrt - hi.astype(F32)).astype(BF16)
    return jnp.concatenate([hi, lo], axis=0)


def _rope_tables(seq):
    pos = jnp.arange(seq, dtype=F32)
    inv_freq = ROPE_THETA ** (-jnp.arange(0, HEAD_DIM, 2, dtype=F32) / HEAD_DIM)
    ang = pos[:, None] * inv_freq[None, :]
    c, s = jnp.cos(ang), jnp.sin(ang)
    cos = jnp.concatenate([c, c, c, c], axis=1)
    sin = jnp.concatenate([-s, s, -s, s], axis=1)
    return cos, sin


def _token_mixer(x2d, batch, seq, p, cnt_in):
    cos, sin = _rope_tables(seq)
    f, qz, k, vt = _inproj(x2d, p["nmix"], p["w1"], cos, sin, tm=INPROJ_TOKENS)
    four = _fourier(f, batch, seq)
    attn = _attention(qz, k, vt, p["lam"], p["sub"], batch, seq, tq=ATTN_QUERIES,
                      ck=min(ATTN_KEYS_MAX, seq // 2), sb=ATTN_SUB_KEYS, kv_block=seq)
    return _merge(x2d, four, attn, p["nmix"], p["wgate"], p["bg"], p["wf"], p["wa"], p["wo"],
                  p["nffn"], p["wr"], cnt_in, tm=MERGE_TOKENS)


def kernel(x_prompt, x_sample, norm_mix_w, w_in, b_gate, w_fourier, lambda_q1, lambda_k1, lambda_q2,
           lambda_k2, subln_w, w_attn, w_out, norm_ffn_w, w_group_router, w_expert_router,
           w_expert_gate, w_expert_up, w_expert_down, norm_final_w):
    wr = _split_router_weight(w_group_router[0], w_expert_router[0])
    p = {
        "nmix": norm_mix_w[0][None, :],
        "w1": w_in[0][:, :MIX_WIDTH].astype(BF16),
        "wgate": w_in[0][:, MIX_WIDTH:].astype(BF16),
        "bg": b_gate[0][None, :],
        "wf": w_fourier[0].astype(BF16),
        "wa": w_attn[0].astype(BF16),
        "wo": w_out[0].astype(BF16),
        "nffn": norm_ffn_w[0][None, :],
        "wr": wr,
        "lam": jnp.stack([lambda_q1[0], lambda_k1[0], lambda_q2[0], lambda_k2[0]]),
        "sub": subln_w[0][:, None],
    }
    inputs = [x_prompt, x_sample]
    mixed = []
    cnt = jnp.zeros((1, ROUTER_LANES), F32)
    for x in inputs:
        batch, seq, _ = x.shape
        h, hnp, ri, rw, cnt = _token_mixer(x.reshape(batch * seq, D_MODEL), batch, seq, p, cnt)
        mixed.append((h, hnp, ri, rw))
    outs = _moe_and_final(mixed, cnt, w_expert_gate[0], w_expert_up[0], w_expert_down[0], norm_final_w)
    return tuple(o.reshape(x.shape) for o, x in zip(outs, inputs))


def _moe_and_final(mixed, cnt, w_gate, w_up, w_down, norm_final_w):
    n_assign = 2 * sum(m[0].shape[0] for m in mixed)
    n_blocks = -(-(n_assign + N_EXPERTS * (MOE_ROWS - 1)) // MOE_ROWS)
    n_rows = n_blocks * MOE_ROWS
    counts = cnt[0, EXPERT_LANE0:EXPERT_LANE0 + N_EXPERTS].astype(I32)
    padded = (counts + MOE_ROWS - 1) // MOE_ROWS * MOE_ROWS
    pend = jnp.cumsum(padded)
    pstart = pend - padded
    total = pend[-1]
    bstart = jnp.arange(n_blocks, dtype=I32) * MOE_ROWS
    be = jnp.minimum(jnp.sum((pend[None, :] <= bstart[:, None]).astype(I32), axis=1), N_EXPERTS - 1)
    n_used = (total // MOE_ROWS).astype(I32)
    be = jnp.where(bstart < total, be, be[jnp.maximum(n_used - 1, 0)])
    block_ord = jnp.cumsum(jnp.concatenate([jnp.zeros((1,), I32), (be[1:] != be[:-1]).astype(I32)]))
    eidx = jnp.arange(N_EXPERTS, dtype=I32)
    later_used = (eidx[None, :] > eidx[:, None]) & (padded[None, :] > 0)
    next_expert = jnp.min(jnp.where(later_used, eidx[None, :], N_EXPERTS), axis=1)
    next_expert = jnp.where(next_expert == N_EXPERTS, -1, next_expert)
    block_next = next_expert[be]
    zstart = pstart + counts
    zlen = jnp.concatenate([(pend - zstart).astype(I32), n_used[None]])

    def row_of(expert, rank):
        hit = expert[None].astype(I32) == eidx[:, None, None]
        return jnp.sum(jnp.where(hit, pstart[:, None, None], 0), axis=0) + rank.astype(I32)

    dests = [row_of(m[2][0:2], m[2][2:4]) for m in mixed]
    xs = _dispatch(zstart, zlen, dests, [m[1] for m in mixed], n_rows, tm=DISPATCH_TOKENS)
    ys = _experts(be, n_used[None], block_ord, block_next, xs, w_gate, w_up, w_down)
    return [_combine(dest, ys, m[0], m[3], norm_final_w[None, :], tm=COMBINE_TOKENS)
            for m, dest in zip(mixed, dests)]
```
